```python
import math
import jax
import jax.numpy as jnp
from jax import lax
import numpy as np

D_MODEL = 1024
BATCH = 16
SEQ = 256
DEPTH = 4
DEC_BATCH = 8
DEC_SEQ = 4096
PAST_LEN = 256

GRID_W = 64
N_MIXERS = 3
N_ATTN_LAYERS = (DEPTH + 2) // 3
N_SSD_LAYERS = (DEPTH + 1) // 3
N_GDN_LAYERS = DEPTH // 3
DN_ALPHA = (2.0 * DEPTH) ** 0.25
DN_BETA = (8.0 * DEPTH) ** -0.25
EPS = 1e-6
A_HEADS = 8
A_HD = 64
A_VD = 2 * A_HD
A_IN = 3 * A_HEADS * 2 * A_HD
Q_BLOCK = 128
ROPE_BASE = 10000.0
SSD_INNER = 2 * D_MODEL
SSD_HD = 64
SSD_HEADS = SSD_INNER // SSD_HD
SSD_GROUPS = 8
SSD_STATE = 128
SSD_CONV_CH = SSD_INNER + 2 * SSD_GROUPS * SSD_STATE
SSD_IN = SSD_INNER + SSD_CONV_CH + 2 * SSD_HEADS
GDN_HEADS = 8
GDN_DK = 128
GDN_DV = 128
GDN_QKV = 2 * GDN_HEADS * GDN_DK + GDN_HEADS * GDN_DV
GDN_IN = GDN_QKV + GDN_HEADS * GDN_DV + 4 * GDN_HEADS
CONV_W = 5
CHUNK = 64
MOE_GROUPS = 4
MOE_PER_GROUP = 4
MOE_EXPERTS = MOE_GROUPS * MOE_PER_GROUP
MOE_TOPK = 2
MOE_FF = 256

kernel_name = 'hybrid_diffattn_ssd_gdn_hmoe_diffusion_step'


def layer_norm(x, g, b):
    xf = x.astype(jnp.float32)
    mu = jnp.mean(xf, -1, keepdims=True)
    var = jnp.mean(jnp.square(xf - mu), -1, keepdims=True)
    return ((xf - mu) * lax.rsqrt(var + EPS) * g.astype(jnp.float32) + b.astype(jnp.float32)).astype(x.dtype)


def rms_norm(x, w):
    xf = x.astype(jnp.float32)
    y = xf * lax.rsqrt(jnp.mean(jnp.square(xf), -1, keepdims=True) + EPS)
    return (y * w.astype(jnp.float32)).astype(x.dtype)


def l2_norm(x):
    xf = x.astype(jnp.float32)
    return xf * lax.rsqrt(jnp.sum(jnp.square(xf), -1, keepdims=True) + EPS)


def centred_dwconv(x, w):
    return lax.conv_general_dilated(x, w[:, None, :], window_strides=(1,),
                                    padding=[(CONV_W // 2, CONV_W // 2)],
                                    dimension_numbers=('NWC', 'WIO', 'NWC'),
                                    feature_group_count=x.shape[-1])


def to_chunks(x):
    b, s = x.shape[:2]
    return jnp.moveaxis(x.reshape((b, s // CHUNK, CHUNK) + x.shape[2:]), 1, 0)


def from_chunks(x):
    x = jnp.moveaxis(x, 0, 1)
    return x.reshape((x.shape[0], x.shape[1] * x.shape[2]) + x.shape[3:])


def modulation(cond, w, b):
    m = jax.nn.silu(cond) @ w + b
    return jnp.split(m[:, None, :], 6, axis=-1)


def grid_positions(n_tokens):
    rows = n_tokens // GRID_W
    r = jnp.repeat(jnp.arange(rows, dtype=jnp.float32), GRID_W)
    col = jnp.tile(jnp.arange(GRID_W, dtype=jnp.float32), rows)
    return r, col


def axial_rope(x, row, col):
    half = A_HD // 2
    quarter = half // 2
    inv_freq = ROPE_BASE ** (-jnp.arange(quarter, dtype=jnp.float32) / quarter)

    def rot(xa, pos):
        ang = pos[:, None] * inv_freq
        cos = jnp.cos(ang)[None, :, None, None, :]
        sin = jnp.sin(ang)[None, :, None, None, :]
        x1, x2 = xa[..., :quarter], xa[..., quarter:]
        return jnp.concatenate([x1 * cos - x2 * sin, x1 * sin + x2 * cos], axis=-1)

    out = jnp.concatenate([rot(x[..., :half], row), rot(x[..., half:], col)], axis=-1)
    return out.astype(x.dtype)


def diff_lambda(lam_p, layer_idx):
    lam_init = 0.8 - 0.6 * math.exp(-0.3 * layer_idx)
    lp = lam_p.astype(jnp.float32)
    lam = jnp.exp(jnp.sum(lp[0] * lp[1])) - jnp.exp(jnp.sum(lp[2] * lp[3])) + lam_init
    return lam, lam_init


def diff_attn_qkv(h, w_in):
    b, s, _ = h.shape
    q, k, v = jnp.split(h @ w_in, 3, axis=-1)
    return (q.reshape(b, s, A_HEADS, 2, A_HD), k.reshape(b, s, A_HEADS, 2, A_HD),
            v.reshape(b, s, A_HEADS, A_VD))


def diff_attn_core(q, k, v, lam):
    b, sq = q.shape[:2]
    qb = jnp.moveaxis(q.reshape(b, sq // Q_BLOCK, Q_BLOCK, A_HEADS, 2, A_HD), 1, 0)

    def one_block(qi):
        s = jnp.einsum('bqhcd,bkhcd->bhcqk', qi, k).astype(jnp.float32) * (A_HD ** -0.5)
        p = jax.nn.softmax(s, axis=-1)
        pd = (p[:, :, 0] - lam * p[:, :, 1]).astype(v.dtype)
        return jnp.einsum('bhqk,bkhe->bqhe', pd, v)

    o = lax.map(one_block, qb)
    return jnp.moveaxis(o, 0, 1).reshape(b, sq, A_HEADS, A_VD)


def diff_attn_out(o, subln, lam_init, w_out):
    b, s = o.shape[:2]
    o = rms_norm(o, subln) * (1.0 - lam_init)
    return o.reshape(b, s, A_HEADS * A_VD) @ w_out


def ssd_scan(x, dt, a, bm, cm, state0):
    f32 = jnp.float32
    bsz = x.shape[0]
    hg = SSD_HEADS // SSD_GROUPS
    inputs = tuple(to_chunks(t.astype(f32)) for t in (x, dt, bm, cm))
    idx = jnp.arange(CHUNK)
    incl = (idx[:, None] >= idx[None, :])[None, :, :, None]

    def step(state, inp):
        xq, dq, bq, cq = inp
        cs = jnp.cumsum(dq * a, axis=1)
        seg = cs[:, :, None, :] - cs[:, None, :, :]
        lmat = jnp.exp(jnp.where(incl, seg, -jnp.inf))
        cb = jnp.einsum('bign,bjgn->bijg', cq, bq)
        m = (lmat * dq[:, None]).reshape(bsz, CHUNK, CHUNK, SSD_GROUPS, hg) * cb[..., None]
        xg = xq.reshape(bsz, CHUNK, SSD_GROUPS, hg, SSD_HD)
        sg = state.reshape(bsz, SSD_GROUPS, hg, SSD_HD, SSD_STATE)
        y = jnp.einsum('bijgh,bjghp->bighp', m, xg)
        y = y + jnp.einsum('bign,bghpn->bighp', cq, sg) * jnp.exp(cs).reshape(bsz, CHUNK, SSD_GROUPS, hg)[..., None]
        w_end = (jnp.exp(cs[:, -1:] - cs) * dq).reshape(bsz, CHUNK, SSD_GROUPS, hg)
        sg = (jnp.exp(cs[:, -1]).reshape(bsz, SSD_GROUPS, hg)[..., None, None] * sg
              + jnp.einsum('bjgh,bjgn,bjghp->bghpn', w_end, bq, xg))
        return sg.reshape(bsz, SSD_HEADS, SSD_HD, SSD_STATE), y.reshape(bsz, CHUNK, SSD_HEADS, SSD_HD)

    s_fin, y = lax.scan(step, state0.astype(f32), inputs)
    return from_chunks(y), s_fin


def ssd_mixer(h, w_in, conv_w, conv_b, a_log, dt_bias, d_skip, norm_w, w_out, state0):
    b, s, _ = h.shape
    f32 = jnp.float32
    z, xbc, dt = jnp.split(h @ w_in, [SSD_INNER, SSD_INNER + SSD_CONV_CH], axis=-1)
    xbc = jax.nn.silu(centred_dwconv(xbc, conv_w) + conv_b)
    xs, bm, cm = jnp.split(xbc, [SSD_INNER, SSD_INNER + SSD_GROUPS * SSD_STATE], axis=-1)
    xs = xs.reshape(b, s, SSD_HEADS, SSD_HD)
    bm = bm.reshape(b, s, SSD_GROUPS, SSD_STATE)
    cm = cm.reshape(b, s, SSD_GROUPS, SSD_STATE)
    dt = jax.nn.softplus(dt.astype(f32).reshape(b, s, 2, SSD_HEADS) + dt_bias.astype(f32))
    a = -jnp.exp(a_log.astype(f32))
    y_f, s_f = ssd_scan(xs, dt[:, :, 0], a[0], bm, cm, state0[:, 0])
    y_b, s_b = ssd_scan(jnp.flip(xs, 1), jnp.flip(dt[:, :, 1], 1), a[1], jnp.flip(bm, 1),
                        jnp.flip(cm, 1), state0[:, 1])
    y = (y_f + jnp.flip(y_b, 1)).astype(h.dtype) + d_skip[:, None] * xs
    y = y.reshape(b, s, SSD_INNER) * jax.nn.silu(z)
    y = rms_norm(y.reshape(b, s, SSD_GROUPS, -1), norm_w.reshape(SSD_GROUPS, -1)).reshape(b, s, SSD_INNER)
    return y @ w_out, jnp.stack([s_f, s_b], axis=1).astype(h.dtype)


def gdn_scan(q, k, v, beta, g, state0):
    f32 = jnp.float32
    inputs = tuple(to_chunks(t.astype(f32)) for t in (q, k, v, beta, g))
    idx = jnp.arange(CHUNK)
    incl = idx[:, None] >= idx[None, :]
    strict = idx[:, None] > idx[None, :]
    eye = jnp.eye(CHUNK, dtype=f32)

    def step(state, inp):
        qc, kc, vc, bc, gc = inp
        gcum = jnp.swapaxes(jnp.cumsum(gc, axis=1), 1, 2)
        bt = jnp.swapaxes(bc, 1, 2)
        decay = jnp.exp(jnp.where(incl, gcum[..., :, None] - gcum[..., None, :], -jnp.inf))
        qh, kh, vh = (jnp.swapaxes(t, 1, 2) for t in (qc, kc, vc))
        kk = jnp.einsum('bhid,bhjd->bhij', kh, kh)
        tri = eye + jnp.where(strict, kk * decay, 0.0) * bt[..., :, None]
        rhs = jnp.concatenate([kh * (bt * jnp.exp(gcum))[..., None], vh * bt[..., None]], axis=-1)
        sol = lax.linalg.triangular_solve(tri, rhs, left_side=True, lower=True, unit_diagonal=True)
        u = sol[..., GDN_DK:] - jnp.einsum('bhcd,bhde->bhce', sol[..., :GDN_DK], state)
        qk = jnp.einsum('bhid,bhjd->bhij', qh, kh) * decay
        o = (jnp.einsum('bhid,bhde->bhie', qh * jnp.exp(gcum)[..., None], state)
             + jnp.einsum('bhij,bhje->bhie', qk, u))
        g_last = gcum[..., -1]
        state = (jnp.exp(g_last)[..., None, None] * state
                 + jnp.einsum('bhjd,bhje->bhde', kh * jnp.exp(g_last[..., None] - gcum)[..., None], u))
        return state, jnp.swapaxes(o, 1, 2)

    s_fin, o = lax.scan(step, state0.astype(f32), inputs)
    return from_chunks(o), s_fin


def gdn_mixer(h, w_in, conv_w, a_log, dt_bias, norm_w, w_out, state0):
    b, s, _ = h.shape
    f32 = jnp.float32
    qkv, z, ab = jnp.split(h @ w_in, [GDN_QKV, GDN_QKV + GDN_HEADS * GDN_DV], axis=-1)
    qkv = jax.nn.silu(centred_dwconv(qkv, conv_w))
    q, k, v = jnp.split(qkv, [GDN_HEADS * GDN_DK, 2 * GDN_HEADS * GDN_DK], axis=-1)
    q = l2_norm(q.reshape(b, s, GDN_HEADS, GDN_DK)) * (GDN_DK ** -0.5)
    k = l2_norm(k.reshape(b, s, GDN_HEADS, GDN_DK))
    v = v.reshape(b, s, GDN_HEADS, GDN_DV)
    ab = ab.astype(f32).reshape(b, s, 2, 2, GDN_HEADS)
    g = -jnp.exp(a_log.astype(f32)) * jax.nn.softplus(ab[:, :, 0] + dt_bias.astype(f32))
    beta = jax.nn.sigmoid(ab[:, :, 1])
    o_f, s_f = gdn_scan(q, k, v, beta[:, :, 0], g[:, :, 0], state0[:, 0])
    o_b, s_b = gdn_scan(jnp.flip(q, 1), jnp.flip(k, 1), jnp.flip(v, 1), jnp.flip(beta[:, :, 1], 1),
                        jnp.flip(g[:, :, 1], 1), state0[:, 1])
    o = (o_f + jnp.flip(o_b, 1)).astype(h.dtype)
    o = rms_norm(o, norm_w) * jax.nn.silu(z.reshape(b, s, GDN_HEADS, GDN_DV))
    return o.reshape(b, s, GDN_HEADS * GDN_DV) @ w_out, jnp.stack([s_f, s_b], axis=1).astype(h.dtype)


def hier_moe(h, w_rg, b_rg, w_re, b_re, w_gate, w_up, w_down):
    f32 = jnp.float32
    shape = h.shape
    t = h.reshape(-1, shape[-1])
    n_tok = t.shape[0]
    pg = jax.nn.softmax((t @ w_rg).astype(f32) + b_rg.astype(f32), axis=-1)
    pg_top, g_top = lax.top_k(pg, 1)
    le = ((t @ w_re).astype(f32) + b_re.astype(f32)).reshape(n_tok, MOE_GROUPS, MOE_PER_GROUP)
    le = le[jnp.arange(n_tok), g_top[:, 0]]
    pe_top, e_top = lax.top_k(jax.nn.softmax(le, axis=-1), MOE_TOPK)
    w_tok = pg_top * pe_top / jnp.sum(pe_top, axis=-1, keepdims=True)
    eid = g_top * MOE_PER_GROUP + e_top
    gate = jnp.sum(jax.nn.one_hot(eid, MOE_EXPERTS, dtype=f32) * w_tok[..., None], axis=1)
    hid = jax.nn.silu(jnp.einsum('td,edf->tef', t, w_gate)) * jnp.einsum('td,edf->tef', t, w_up)
    y = jnp.einsum('tef,efd->td', hid * gate[..., None].astype(hid.dtype), w_down)
    return y.reshape(shape)


def setup_inputs(seed: int = 0) -> dict:
    key = jax.random.key(seed)
    keys = iter(jax.random.split(key, 64))
    f32 = jnp.float32
    D = D_MODEL

    def nrm(shape, scale=1.0):
        return scale * jax.random.normal(next(keys), shape, f32)

    def gain(shape):
        return 1.0 + nrm(shape, 0.02)

    def dt_bias(shape):
        dt0 = jnp.exp(jax.random.uniform(next(keys), shape, f32, math.log(1e-3), math.log(1e-1)))
        return dt0 + jnp.log(-jnp.expm1(-dt0))

    def a_log(shape):
        return jnp.log(jax.random.uniform(next(keys), shape, f32, 1.0, 16.0))

    return {
        'x_prompt': nrm((BATCH, SEQ, D)),
        'x_sample': nrm((DEC_BATCH, DEC_SEQ, D)),
        'cache_k': nrm((DEC_BATCH, N_ATTN_LAYERS, PAST_LEN, A_HEADS, 2, A_HD)),
        'cache_v': nrm((DEC_BATCH, N_ATTN_LAYERS, PAST_LEN, A_HEADS, A_VD)),
        'state_ssd': nrm((DEC_BATCH, N_SSD_LAYERS, 2, SSD_HEADS, SSD_HD, SSD_STATE), 0.5),
        'state_delta': nrm((DEC_BATCH, N_GDN_LAYERS, 2, GDN_HEADS, GDN_DK, GDN_DV), 0.5),
        'c': nrm((DEC_BATCH, D)),
        'c_ctx': nrm((D,)),
        'mod_w': nrm((DEPTH, D, 6 * D), 0.5 * D ** -0.5),
        'mod_b': nrm((DEPTH, 6 * D), 0.02),
        'ln_g': gain((DEPTH, 2, D)),
        'ln_b': nrm((DEPTH, 2, D), 0.02),
        'attn_w_in': nrm((N_ATTN_LAYERS, D, A_IN), D ** -0.5),
        'attn_lam': nrm((N_ATTN_LAYERS, 4, A_HD), 0.1),
        'attn_subln': gain((N_ATTN_LAYERS, A_VD)),
        'attn_w_out': nrm((N_ATTN_LAYERS, A_HEADS * A_VD, D), (A_HEADS * A_VD) ** -0.5 * DN_BETA),
        'ssd_w_in': nrm((N_SSD_LAYERS, D, SSD_IN), D ** -0.5),
        'ssd_conv_w': nrm((N_SSD_LAYERS, CONV_W, SSD_CONV_CH), CONV_W ** -0.5),
        'ssd_conv_b': nrm((N_SSD_LAYERS, SSD_CONV_CH), 0.02),
        'ssd_a_log': a_log((N_SSD_LAYERS, 2, SSD_HEADS)),
        'ssd_dt_bias': dt_bias((N_SSD_LAYERS, 2, SSD_HEADS)),
        'ssd_d': 1.0 + nrm((N_SSD_LAYERS, SSD_HEADS), 0.1),
        'ssd_norm': gain((N_SSD_LAYERS, SSD_INNER)),
        'ssd_w_out': nrm((N_SSD_LAYERS, SSD_INNER, D), SSD_INNER ** -0.5 * DN_BETA),
        'gdn_w_in': nrm((N_GDN_LAYERS, D, GDN_IN), D ** -0.5),
        'gdn_conv_w': nrm((N_GDN_LAYERS, CONV_W, GDN_QKV), CONV_W ** -0.5),
        'gdn_a_log': a_log((N_GDN_LAYERS, 2, GDN_HEADS)),
        'gdn_dt_bias': dt_bias((N_GDN_LAYERS, 2, GDN_HEADS)),
        'gdn_norm': gain((N_GDN_LAYERS, GDN_DV)),
        'gdn_w_out': nrm((N_GDN_LAYERS, GDN_HEADS * GDN_DV, D), (GDN_HEADS * GDN_DV) ** -0.5 * DN_BETA),
        'moe_w_rg': nrm((DEPTH, D, MOE_GROUPS), D ** -0.5),
        'moe_b_rg': nrm((DEPTH, MOE_GROUPS), 0.01),
        'moe_w_re': nrm((DEPTH, D, MOE_EXPERTS), D ** -0.5),
        'moe_b_re': nrm((DEPTH, MOE_EXPERTS), 0.01),
        'moe_w_gate': nrm((DEPTH, MOE_EXPERTS, D, MOE_FF), D ** -0.5),
        'moe_w_up': nrm((DEPTH, MOE_EXPERTS, D, MOE_FF), D ** -0.5),
        'moe_w_down': nrm((DEPTH, MOE_EXPERTS, MOE_FF, D), MOE_FF ** -0.5 * DN_BETA),
    }


def reference(x_prompt, x_sample, cache_k, cache_v, state_ssd, state_delta, c, c_ctx,
              mod_w, mod_b, ln_g, ln_b,
              attn_w_in, attn_lam, attn_subln, attn_w_out,
              ssd_w_in, ssd_conv_w, ssd_conv_b, ssd_a_log, ssd_dt_bias, ssd_d, ssd_norm, ssd_w_out,
              gdn_w_in, gdn_conv_w, gdn_a_log, gdn_dt_bias, gdn_norm, gdn_w_out,
              moe_w_rg, moe_b_rg, moe_w_re, moe_b_re, moe_w_gate, moe_w_up, moe_w_down):
    xc, xl = x_prompt, x_sample
    n_ctx_req = xc.shape[0]
    row_pos, col_pos = grid_positions(xl.shape[1])
    new_k, new_v, new_ssd, new_gdn = [], [], [], []
    for i in range(DEPTH):
        kind, j = i % N_MIXERS, i // N_MIXERS
        mc = modulation(c_ctx[None], mod_w[i], mod_b[i])
        ml = modulation(c, mod_w[i], mod_b[i])
        hc = xc * (1 + mc[1]) + mc[0]
        hl = xl * (1 + ml[1]) + ml[0]
        if kind == 0:
            lam, lam_init = diff_lambda(attn_lam[j], i)
            qc, kc, vc = diff_attn_qkv(hc, attn_w_in[j])
            yc = diff_attn_out(diff_attn_core(qc, kc, vc, lam), attn_subln[j], lam_init, attn_w_out[j])
            ql, kl, vl = diff_attn_qkv(hl, attn_w_in[j])
            ql = axial_rope(ql, row_pos, col_pos)
            kl = axial_rope(kl, row_pos, col_pos)
            keys = jnp.concatenate([cache_k[:, j], kl], axis=1)
            vals = jnp.concatenate([cache_v[:, j], vl], axis=1)
            yl = diff_attn_out(diff_attn_core(ql, keys, vals, lam), attn_subln[j], lam_init, attn_w_out[j])
            new_k.append(kc)
            new_v.append(vc)
        elif kind == 1:
            zero_state = jnp.zeros((n_ctx_req, 2, SSD_HEADS, SSD_HD, SSD_STATE), jnp.float32)
            yc, sc = ssd_mixer(hc, ssd_w_in[j], ssd_conv_w[j], ssd_conv_b[j], ssd_a_log[j], ssd_dt_bias[j],
                               ssd_d[j], ssd_norm[j], ssd_w_out[j], zero_state)
            yl, _ = ssd_mixer(hl, ssd_w_in[j], ssd_conv_w[j], ssd_conv_b[j], ssd_a_log[j], ssd_dt_bias[j],
                              ssd_d[j], ssd_norm[j], ssd_w_out[j], state_ssd[:, j])
            new_ssd.append(sc)
        else:
            zero_state = jnp.zeros((n_ctx_req, 2, GDN_HEADS, GDN_DK, GDN_DV), jnp.float32)
            yc, sc = gdn_mixer(hc, gdn_w_in[j], gdn_conv_w[j], gdn_a_log[j], gdn_dt_bias[j],
                               gdn_norm[j], gdn_w_out[j], zero_state)
            yl, _ = gdn_mixer(hl, gdn_w_in[j], gdn_conv_w[j], gdn_a_log[j], gdn_dt_bias[j],
                              gdn_norm[j], gdn_w_out[j], state_delta[:, j])
            new_gdn.append(sc)
        xc = layer_norm(DN_ALPHA * xc + mc[2] * yc, ln_g[i, 0], ln_b[i, 0])
        xl = layer_norm(DN_ALPHA * xl + ml[2] * yl, ln_g[i, 0], ln_b[i, 0])
        hc = xc * (1 + mc[4]) + mc[3]
        hl = xl * (1 + ml[4]) + ml[3]
        fc = hier_moe(hc, moe_w_rg[i], moe_b_rg[i], moe_w_re[i], moe_b_re[i], moe_w_gate[i], moe_w_up[i], moe_w_down[i])
        fl = hier_moe(hl, moe_w_rg[i], moe_b_rg[i], moe_w_re[i], moe_b_re[i], moe_w_gate[i], moe_w_up[i], moe_w_down[i])
        xc = layer_norm(DN_ALPHA * xc + mc[5] * fc, ln_g[i, 1], ln_b[i, 1])
        xl = layer_norm(DN_ALPHA * xl + ml[5] * fl, ln_g[i, 1], ln_b[i, 1])
    return (xc, xl, jnp.stack(new_k, axis=1), jnp.stack(new_v, axis=1),
            jnp.stack(new_ssd, axis=1), jnp.stack(new_gdn, axis=1))
```

```python
import functools
import math

import jax
import jax.numpy as jnp
from jax import lax
from jax.experimental import pallas as pl
from jax.experimental.pallas import tpu as pltpu

F32 = jnp.float32
BF16 = jnp.bfloat16
HIGHEST = lax.Precision.HIGHEST

D_MODEL = 1024
DEPTH = 4
GRID_W = 64
DN_ALPHA = (2.0 * DEPTH) ** 0.25
EPS = 1e-6
A_HEADS = 8
A_HD = 64
A_VD = 2 * A_HD
ROPE_BASE = 10000.0
SSD_INNER = 2 * D_MODEL
SSD_HD = 64
SSD_HEADS = SSD_INNER // SSD_HD
SSD_GROUPS = 8
SSD_STATE = 128
SSD_GROUP_W = SSD_INNER // SSD_GROUPS
GDN_HEADS = 8
GDN_DK = 128
GDN_DV = 128
CONV_W = 5
CHUNK = 64
MOE_GROUPS = 4
MOE_PER_GROUP = 4
MOE_EXPERTS = MOE_GROUPS * MOE_PER_GROUP
MOE_FF = 256

LANES = 128
HALO = 16
VMEM_LIMIT = 52 * 1024 * 1024


def _params(*sem):
    return pltpu.CompilerParams(dimension_semantics=sem, vmem_limit_bytes=VMEM_LIMIT)


def _silu(v):
    return v * jax.nn.sigmoid(v)


def _softplus(v):
    return jnp.maximum(v, 0.0) + jnp.log(1.0 + jnp.exp(-jnp.abs(v)))


def _layer_norm(v, g, b):
    mu = jnp.mean(v, axis=-1, keepdims=True)
    d = v - mu
    var = jnp.mean(d * d, axis=-1, keepdims=True)
    return d * lax.rsqrt(var + EPS) * g + b


def _dot(a, b):
    return jnp.dot(a, b, preferred_element_type=F32)


def _dot_hi(a, b):
    return jnp.dot(a, b, preferred_element_type=F32, precision=HIGHEST)


def _dot_nt(a, b):
    return lax.dot_general(a, b, (((1,), (1,)), ((), ())), preferred_element_type=F32)


def _dot_tn(a, b):
    return lax.dot_general(a, b, (((0,), (0,)), ((), ())), preferred_element_type=F32)


def _mod_kernel(c_ref, w_ref, b_ref, o_ref):
    o_ref[...] = _dot_hi(_silu(c_ref[...]), w_ref[...]) + b_ref[...]


def _modulations(cond, mod_w, mod_b):
    depth, d, n = mod_w.shape
    r = cond.shape[0]
    tn = 1536
    return pl.pallas_call(
        _mod_kernel,
        grid=(depth, n // tn),
        in_specs=[pl.BlockSpec((r, d), lambda l, j: (0, 0)),
                  pl.BlockSpec((None, d, tn), lambda l, j: (l, 0, j)),
                  pl.BlockSpec((None, 1, tn), lambda l, j: (l, 0, j))],
        out_specs=pl.BlockSpec((None, r, tn), lambda l, j: (l, 0, j)),
        out_shape=jax.ShapeDtypeStruct((depth, r, n), F32),
        compiler_params=_params("arbitrary", "arbitrary"),
        name="modulation",
    )(cond, mod_w, mod_b.reshape(depth, 1, n))


def _in_proj_kernel(*refs, n_rope_tiles, tn):
    if n_rope_tiles:
        x_ref, mod_ref, w_ref, cos_ref, sa_ref, sb_ref, o_ref, h_scr = refs
    else:
        x_ref, mod_ref, w_ref, o_ref, h_scr = refs
    j = pl.program_id(1)

    @pl.when(j == 0)
    def _():
        h = x_ref[...] * (1.0 + mod_ref[1:2, :]) + mod_ref[0:1, :]
        h_scr[...] = h.astype(BF16)

    acc = _dot(h_scr[...], w_ref[...])
    if not n_rope_tiles:
        o_ref[...] = acc.astype(o_ref.dtype)
        return

    @pl.when(j < n_rope_tiles)
    def _():
        cos, sa, sb = cos_ref[...], sa_ref[...], sb_ref[...]
        for c in range(tn // LANES):
            a = acc[:, c * LANES:(c + 1) * LANES]
            r = a * cos + pltpu.roll(a, LANES - A_HD // 4, 1) * sa + pltpu.roll(a, A_HD // 4, 1) * sb
            o_ref[:, c * LANES:(c + 1) * LANES] = r.astype(o_ref.dtype)

    @pl.when(j >= n_rope_tiles)
    def _():
        o_ref[...] = acc.astype(o_ref.dtype)


def _in_proj(x, mods, w, *, seq, out_dtype, tm, tn, rope=None, n_rope_cols=0):
    t, d = x.shape
    n = w.shape[1]
    shared = mods.shape[0] == 1
    assert shared or seq % tm == 0
    tiles_per_req = max(seq // tm, 1)
    mod_map = (lambda i, j: (0, 0, 0)) if shared else (lambda i, j: (i // tiles_per_req, 0, 0))
    in_specs = [pl.BlockSpec((tm, d), lambda i, j: (i, 0)),
                pl.BlockSpec((None, 6, d), mod_map),
                pl.BlockSpec((d, tn), lambda i, j: (0, j))]
    args = [x, mods, w]
    n_rope_tiles = 0
    if rope is not None:
        n_rope_tiles = n_rope_cols // tn
        for tab in rope:
            in_specs.append(pl.BlockSpec((tm, LANES), lambda i, j: (i % tiles_per_req, 0)))
            args.append(tab)
    return pl.pallas_call(
        functools.partial(_in_proj_kernel, n_rope_tiles=n_rope_tiles, tn=tn),
        grid=(t // tm, n // tn),
        in_specs=in_specs,
        out_specs=pl.BlockSpec((tm, tn), lambda i, j: (i, j)),
        out_shape=jax.ShapeDtypeStruct((t, n), out_dtype),
        scratch_shapes=[pltpu.VMEM((tm, d), BF16)],
        compiler_params=_params("arbitrary", "arbitrary"),
        name="in_proj",
    )(*args)


def _out_proj_kernel(a_ref, x_ref, mod_ref, w_ref, g_ref, b_ref, o_ref):
    y = _dot(a_ref[...], w_ref[...])
    v = DN_ALPHA * x_ref[...] + mod_ref[2:3, :] * y
    o_ref[...] = _layer_norm(v, g_ref[...], b_ref[...])


def _out_proj(a, x, mods, w, ln_g, ln_b, *, seq, tm):
    t, d = x.shape
    k = a.shape[1]
    shared = mods.shape[0] == 1
    assert shared or seq % tm == 0
    tiles_per_req = max(seq // tm, 1)
    mod_map = (lambda i: (0, 0, 0)) if shared else (lambda i: (i // tiles_per_req, 0, 0))
    return pl.pallas_call(
        _out_proj_kernel,
        grid=(t // tm,),
        in_specs=[pl.BlockSpec((tm, k), lambda i: (i, 0)),
                  pl.BlockSpec((tm, d), lambda i: (i, 0)),
                  pl.BlockSpec((None, 6, d), mod_map),
                  pl.BlockSpec((k, d), lambda i: (0, 0)),
                  pl.BlockSpec((1, d), lambda i: (0, 0)),
                  pl.BlockSpec((1, d), lambda i: (0, 0))],
        out_specs=pl.BlockSpec((tm, d), lambda i: (i, 0)),
        out_shape=jax.ShapeDtypeStruct((t, d), F32),
        compiler_params=_params("arbitrary"),
        name="out_proj",
    )(a, x, mods, w, ln_g.reshape(1, d), ln_b.reshape(1, d))


def _moe_kernel(x_ref, mod_ref, wr_ref, br_ref, wgu_ref, wd_ref, g_ref, b_ref, o_ref,
                t_scr, gate_scr, acc_scr):
    e = pl.program_id(1)
    tm = x_ref.shape[0]
    lane = lax.broadcasted_iota(jnp.int32, (tm, LANES), 1)

    @pl.when(e == 0)
    def _():
        t = x_ref[...] * (1.0 + mod_ref[4:5, :]) + mod_ref[3:4, :]
        t_scr[...] = t.astype(BF16)
        logits = _dot_hi(t, wr_ref[...]) + br_ref[...]
        lane_f = lane.astype(F32)
        neg = jnp.float32(-jnp.inf)
        big = jnp.float32(1e9)
        is_g = (lane >= MOE_EXPERTS) & (lane < MOE_EXPERTS + MOE_GROUPS)
        gl = jnp.where(is_g, logits, neg)
        gmax = jnp.max(gl, axis=-1, keepdims=True)
        pg_top = 1.0 / jnp.sum(jnp.exp(gl - gmax), axis=-1, keepdims=True)
        g_top = jnp.min(jnp.where(gl == gmax, lane_f, big), axis=-1, keepdims=True) - MOE_EXPERTS
        in_grp = (lane < MOE_EXPERTS) & ((lane // MOE_PER_GROUP).astype(F32) == g_top)
        el = jnp.where(in_grp, logits, neg)
        m1 = jnp.max(el, axis=-1, keepdims=True)
        i1 = jnp.min(jnp.where(el == m1, lane_f, big), axis=-1, keepdims=True)
        el2 = jnp.where(lane_f == i1, neg, el)
        m2 = jnp.max(el2, axis=-1, keepdims=True)
        i2 = jnp.min(jnp.where(el2 == m2, lane_f, big), axis=-1, keepdims=True)
        r = jnp.exp(m2 - m1)
        w1 = pg_top / (1.0 + r)
        w2 = pg_top * r / (1.0 + r)
        gate_scr[...] = jnp.where(lane_f == i1, w1, 0.0) + jnp.where(lane_f == i2, w2, 0.0)
        acc_scr[...] = jnp.zeros_like(acc_scr)

    hgu = _dot(t_scr[...], wgu_ref[...])
    g_e = jnp.sum(jnp.where(lane == e, gate_scr[...], 0.0), axis=-1, keepdims=True)
    hid = _silu(hgu[:, :MOE_FF]) * hgu[:, MOE_FF:] * g_e
    acc_scr[...] += _dot(hid.astype(BF16), wd_ref[...])

    @pl.when(e == MOE_EXPERTS - 1)
    def _():
        v = DN_ALPHA * x_ref[...] + mod_ref[5:6, :] * acc_scr[...]
        o_ref[...] = _layer_norm(v, g_ref[...], b_ref[...])


def _moe(x, mods, w_router, b_router, w_gu, w_d, ln_g, ln_b, *, seq, tm):
    t, d = x.shape
    shared = mods.shape[0] == 1
    assert shared or seq % tm == 0
    tiles_per_req = max(seq // tm, 1)
    mod_map = (lambda i, e: (0, 0, 0)) if shared else (lambda i, e: (i // tiles_per_req, 0, 0))
    return pl.pallas_call(
        _moe_kernel,
        grid=(t // tm, MOE_EXPERTS),
        in_specs=[pl.BlockSpec((tm, d), lambda i, e: (i, 0)),
                  pl.BlockSpec((None, 6, d), mod_map),
                  pl.BlockSpec((d, LANES), lambda i, e: (0, 0)),
                  pl.BlockSpec((1, LANES), lambda i, e: (0, 0)),
                  pl.BlockSpec((None, d, 2 * MOE_FF), lambda i, e: (e, 0, 0)),
                  pl.BlockSpec((None, MOE_FF, d), lambda i, e: (e, 0, 0)),
                  pl.BlockSpec((1, d), lambda i, e: (0, 0)),
                  pl.BlockSpec((1, d), lambda i, e: (0, 0))],
        out_specs=pl.BlockSpec((tm, d), lambda i, e: (i, 0)),
        out_shape=jax.ShapeDtypeStruct((t, d), F32),
        scratch_shapes=[pltpu.VMEM((tm, d), BF16), pltpu.VMEM((tm, LANES), F32), pltpu.VMEM((tm, d), F32)],
        compiler_params=_params("arbitrary", "arbitrary"),
        name="moe",
    )(x, mods, w_router, b_router, w_gu, w_d, ln_g.reshape(1, d), ln_b.reshape(1, d))


def _attn_kernel(*refs, n_seq_tiles, tk, tw, has_cache, lam_init):
    if has_cache:
        q_ref, k_ref, v_ref, ck_ref, cv_ref, lam_ref, sub_ref, o_ref, kt_scr, vb_scr, s_scr, acc_scr = refs
    else:
        q_ref, k_ref, v_ref, lam_ref, sub_ref, o_ref, kt_scr, vb_scr, s_scr, acc_scr = refs
    tq = q_ref.shape[0]
    off = tk if has_cache else 0
    n_keys = n_seq_tiles * tk + off

    @pl.when(pl.program_id(2) == 0)
    def _():
        if has_cache:
            kt_scr[:, 0:tk] = ck_ref[...].T.astype(BF16)
            vb_scr[0:tk, :] = cv_ref[...].astype(BF16)
        for i in range(n_seq_tiles):
            kt_scr[:, off + i * tk:off + (i + 1) * tk] = k_ref[i * tk:(i + 1) * tk, :].astype(F32).T.astype(BF16)
            vb_scr[off + i * tk:off + (i + 1) * tk, :] = v_ref[i * tk:(i + 1) * tk, :].astype(BF16)

    lane = lax.broadcasted_iota(jnp.int32, (tq, LANES), 1)
    q = q_ref[...].astype(F32) * (A_HD ** -0.5)
    qs = (jnp.where(lane < A_HD, q, 0.0).astype(BF16), jnp.where(lane >= A_HD, q, 0.0).astype(BF16))

    tiles = [(a, min(a + tw, n_keys)) for a in range(0, n_keys, tw)]
    ms = [jnp.full((tq, 1), -jnp.inf, F32)] * 2
    for a, b in tiles:
        k_tile = kt_scr[:, a:b]
        for c in range(2):
            s = _dot(qs[c], k_tile)
            s_scr[c, :, a:b] = s
            ms[c] = jnp.maximum(ms[c], jnp.max(s, axis=-1, keepdims=True))

    ls = [jnp.zeros((tq, 1), F32)] * 2
    for i, (a, b) in enumerate(tiles):
        v_tile = vb_scr[a:b, :]
        for c in range(2):
            p = jnp.exp(s_scr[c, :, a:b] - ms[c])
            ls[c] = ls[c] + jnp.sum(p, axis=-1, keepdims=True)
            pv = _dot(p.astype(BF16), v_tile)
            if i == 0:
                acc_scr[c] = pv
            else:
                acc_scr[c] += pv

    lp = lam_ref[...]
    lam = (jnp.exp(jnp.sum(lp[0:1, :] * lp[1:2, :], axis=-1, keepdims=True))
           - jnp.exp(jnp.sum(lp[2:3, :] * lp[3:4, :], axis=-1, keepdims=True)) + lam_init)
    o = acc_scr[0] / ls[0] - lam * (acc_scr[1] / ls[1])
    o = o * lax.rsqrt(jnp.mean(o * o, axis=-1, keepdims=True) + EPS) * sub_ref[...]
    o_ref[...] = (o * (1.0 - lam_init)).astype(o_ref.dtype)


def _attention(qkv, cache_k, cache_v, lam_p, subln, *, batch, seq, tq, tk, lam_init):
    has_cache = cache_k is not None
    hw = A_HEADS * A_VD
    nq = seq // tq
    n_seq_tiles = seq // tk
    in_specs = [pl.BlockSpec((tq, A_VD), lambda b, h, i: (b * nq + i, h)),
                pl.BlockSpec((seq, A_VD), lambda b, h, i: (b, A_HEADS + h)),
                pl.BlockSpec((seq, A_VD), lambda b, h, i: (b, 2 * A_HEADS + h))]
    args = [qkv, qkv, qkv]
    n_tiles = n_seq_tiles
    if has_cache:
        past = cache_k.shape[1]
        assert past == tk
        in_specs += [pl.BlockSpec((None, past, A_VD), lambda b, h, i: (b, 0, h)),
                     pl.BlockSpec((None, past, A_VD), lambda b, h, i: (b, 0, h))]
        args += [cache_k, cache_v]
        n_tiles += 1
    in_specs += [pl.BlockSpec((4, A_HD), lambda b, h, i: (0, 0)),
                 pl.BlockSpec((1, A_VD), lambda b, h, i: (0, 0))]
    args += [lam_p, subln.reshape(1, A_VD)]
    return pl.pallas_call(
        functools.partial(_attn_kernel, n_seq_tiles=n_seq_tiles, tk=tk, tw=tk, has_cache=has_cache,
                          lam_init=lam_init),
        grid=(batch, A_HEADS, nq),
        in_specs=in_specs,
        out_specs=pl.BlockSpec((tq, A_VD), lambda b, h, i: (b * nq + i, h)),
        out_shape=jax.ShapeDtypeStruct((batch * seq, hw), BF16),
        scratch_shapes=[pltpu.VMEM((A_VD, n_tiles * tk), BF16), pltpu.VMEM((n_tiles * tk, A_VD), BF16),
                        pltpu.VMEM((2, tq, n_tiles * tk), F32), pltpu.VMEM((2, tq, A_VD), F32)],
        compiler_params=_params("arbitrary", "arbitrary", "arbitrary"),
        name="diff_attention",
    )(*args)


def _rope_tables(seq):
    quarter = A_HD // 4
    inv_freq = ROPE_BASE ** (-jnp.arange(quarter, dtype=F32) / quarter)
    pos = jnp.arange(seq)
    row = (pos // GRID_W).astype(F32)
    col = (pos % GRID_W).astype(F32)
    lane = jnp.arange(LANES)
    d = lane % A_HD
    p = jnp.where((d < A_HD // 2)[None, :], row[:, None], col[:, None])
    ang = p * inv_freq[d % quarter][None, :]
    first = ((d % (A_HD // 2)) < quarter)[None, :]
    cos, sin = jnp.cos(ang), jnp.sin(ang)
    return cos, jnp.where(first, -sin, 0.0), jnp.where(first, 0.0, sin)


def _conv_silu_chunk(raw_ref, w_ref, bias, dst_ref, win_scr, c, n_chunks):
    width = raw_ref.shape[1]
    seq = n_chunks * CHUNK
    r0 = pl.multiple_of(c * CHUNK, CHUNK)
    prev = raw_ref[pl.ds(pl.multiple_of(jnp.maximum(r0 - HALO, 0), HALO), HALO), :].astype(F32)
    nxt = raw_ref[pl.ds(pl.multiple_of(jnp.minimum(r0 + CHUNK, seq - HALO), HALO), HALO), :].astype(F32)
    win_scr[0:HALO, 0:width] = jnp.where(c > 0, prev, 0.0)
    win_scr[HALO:HALO + CHUNK, 0:width] = raw_ref[pl.ds(r0, CHUNK), :].astype(F32)
    win_scr[HALO + CHUNK:, 0:width] = jnp.where(c < n_chunks - 1, nxt, 0.0)
    acc = jnp.zeros((CHUNK, width), F32) + bias
    for k in range(CONV_W):
        s = HALO - CONV_W // 2 + k
        acc = acc + w_ref[k:k + 1, :] * win_scr[s:s + CHUNK, 0:width]
    dst_ref[pl.ds(r0, CHUNK), :] = _silu(acc).astype(dst_ref.dtype)


def _tri(n, upper):
    i = lax.broadcasted_iota(jnp.int32, (n, n), 0)
    j = lax.broadcasted_iota(jnp.int32, (n, n), 1)
    return ((i <= j) if upper else (i >= j)).astype(F32)


def _split3(v):
    hi = v.astype(BF16)
    r1 = v - hi.astype(F32)
    mid = r1.astype(BF16)
    lo = (r1 - mid.astype(F32)).astype(BF16)
    return hi, mid, lo


def _chunk_sums(v):
    rows = v.shape[0]
    i = lax.broadcasted_iota(jnp.int32, (rows, rows), 0)
    j = lax.broadcasted_iota(jnp.int32, (rows, rows), 1)
    same = (i // CHUNK) == (j // CHUNK)
    lower = (same & (i >= j)).astype(BF16)
    upper = (same & (i <= j)).astype(BF16)
    pieces = _split3(v)
    return sum(_dot(lower, p) for p in pieces), sum(_dot(upper, p) for p in pieces)


def _ssd_gates_kernel(dt_ref, par_ref, o_ref):
    nh = 2 * SSD_HEADS
    dt = _softplus(dt_ref[...] + par_ref[1:2, :])
    pre, suf = _chunk_sums(dt * -jnp.exp(par_ref[0:1, :]))
    lane = lax.broadcasted_iota(jnp.int32, dt.shape, 1)
    cs = pltpu.roll(jnp.where(lane < SSD_HEADS, pre, suf), nh, 1)
    o_ref[...] = jnp.where(lane < nh, dt, cs)


def _ssd_gates(dt_raw, par):
    t, w = dt_raw.shape
    rows = 4 * CHUNK
    return pl.pallas_call(
        _ssd_gates_kernel,
        grid=(t // rows,),
        in_specs=[pl.BlockSpec((rows, w), lambda i: (i, 0)), pl.BlockSpec((2, w), lambda i: (0, 0))],
        out_specs=pl.BlockSpec((rows, w), lambda i: (i, 0)),
        out_shape=jax.ShapeDtypeStruct((t, w), F32),
        compiler_params=_params("arbitrary"),
        name="ssd_gates",
    )(dt_raw, par)


def _ssd_kernel(z_ref, x_ref, b_ref, c_ref, gates_ref, gt_ref, cwx_ref, cwb_ref, cwc_ref, cbx_ref, cbb_ref, cbc_ref,
                dskip_ref, nw_ref, s0_ref, y_ref, sout_ref,
                xc_scr, bc_scr, cc_scr, yf_scr, yb_scr, win_scr, *, n_chunks, unroll):
    g = pl.program_id(1)
    gw = SSD_GROUP_W
    hpg = SSD_HEADS // SSD_GROUPS
    nh = 2 * SSD_HEADS

    def conv_body(c, carry):
        _conv_silu_chunk(x_ref, cwx_ref, cbx_ref[...], xc_scr, win_scr, c, n_chunks)
        _conv_silu_chunk(b_ref, cwb_ref, cbb_ref[...], bc_scr, win_scr, c, n_chunks)
        _conv_silu_chunk(c_ref, cwc_ref, cbc_ref[...], cc_scr, win_scr, c, n_chunks)
        return carry

    lax.fori_loop(0, n_chunks, conv_body, 0)

    ii = lax.broadcasted_iota(jnp.int32, (CHUNK, gw), 0)
    jj = lax.broadcasted_iota(jnp.int32, (CHUNK, gw), 1) % CHUNK
    masks = ((ii >= jj), (ii <= jj))
    bi = lax.broadcasted_iota(jnp.int32, (gw, gw), 0)
    bj = lax.broadcasted_iota(jnp.int32, (gw, gw), 1)
    blockdiag = (bi // SSD_HD) == (bj // SSD_HD)
    sr = lax.broadcasted_iota(jnp.int32, (2 * nh, 2 * gw), 0)
    sl = lax.broadcasted_iota(jnp.int32, (2 * nh, 2 * gw), 1)
    sels = [(sr == (sl // gw) * nh + d * SSD_HEADS + g * hpg + (sl % gw) // SSD_HD).astype(BF16) for d in range(2)]
    sels = [jnp.concatenate([s, s], axis=0) for s in sels]

    def step(t, sts):
        chains = []
        for k in range(unroll):
            for d in range(2):
                c = t * unroll + k if d == 0 else n_chunks - 1 - (t * unroll + k)
                rows = pl.ds(pl.multiple_of(c * CHUNK, CHUNK), CHUNK)
                base = d * hpg
                dt_r = jnp.concatenate([gt_ref[c, base + hh:base + hh + 1, :] for hh in range(hpg)], axis=1)
                cs_r = jnp.concatenate([gt_ref[c, 2 * hpg + base + hh:2 * hpg + base + hh + 1, :]
                                        for hh in range(hpg)], axis=1)
                chains.append(dict(d=d, rows=rows, gt=gates_ref[rows, :], dt_r=dt_r, cs_r=cs_r,
                                   xq=xc_scr[rows, :], bq=bc_scr[rows, :], cq=cc_scr[rows, :]))
        for cn in chains:
            e = _dot(jnp.concatenate(_hi_lo(cn["gt"]), axis=1), sels[cn["d"]])
            cn["dt_e"], cn["cs_e"] = e[:, :gw], e[:, gw:]
            cn["cb"] = _dot_nt(cn["cq"], jnp.concatenate([cn["bq"]] * hpg, axis=0))
            cn["bt"] = cn["bq"].astype(F32).T.astype(BF16)
        for cn in chains:
            d, cs_e = cn["d"], cn["cs_e"]
            last = CHUNK - 1 if d == 0 else 0
            cs_last = cs_e[last:last + 1, :]
            m = jnp.where(masks[d], jnp.exp(cs_e - cn["cs_r"]), 0.0) * cn["dt_r"] * cn["cb"]
            xbd = jnp.concatenate([cn["xq"].astype(BF16)] * hpg, axis=0)
            xbd = jnp.where(blockdiag, xbd, jnp.zeros_like(xbd))
            cn["y"] = _dot(m.astype(BF16), xbd)
            xw = (cn["xq"] * (jnp.exp(cs_last - cs_e) * cn["dt_e"])).astype(BF16)
            cn["upd"] = _dot(cn["bt"], xw)
            cn["dec"] = jnp.exp(cs_last)
            cn["ecs"] = jnp.exp(cs_e)
        sts = list(sts)
        for cn in chains:
            d = cn["d"]
            y = cn["y"] + _dot(cn["cq"], sts[d].astype(BF16)) * cn["ecs"]
            sts[d] = sts[d] * cn["dec"] + cn["upd"]
            if d == 0:
                yf_scr[cn["rows"], :] = y
            else:
                yb_scr[cn["rows"], :] = y
        return tuple(sts)

    sts = lax.fori_loop(0, n_chunks // unroll, step, (s0_ref[0].T, s0_ref[1].T))

    for d in range(2):
        sout_ref[d] = sts[d].T

    def fin_body(c, carry):
        r0 = pl.multiple_of(c * CHUNK, CHUNK)
        rows = pl.ds(r0, CHUNK)
        y = (yf_scr[rows, :] + yb_scr[rows, :]) + dskip_ref[...] * xc_scr[rows, :]
        y = y * _silu(z_ref[rows, :].astype(F32))
        y = y * lax.rsqrt(jnp.mean(y * y, axis=-1, keepdims=True) + EPS) * nw_ref[...]
        y_ref[rows, :] = y.astype(y_ref.dtype)
        return carry

    lax.fori_loop(0, n_chunks, fin_body, 0)


def _ssd_scan(zx, dt_raw, conv_w, conv_b, par, dskip, norm_w, state0, *, batch, seq):
    gw = SSD_GROUP_W
    hpg = SSD_HEADS // SSD_GROUPS
    n_chunks = seq // CHUNK
    gates = _ssd_gates(dt_raw, par)
    gt = gates.reshape(batch * n_chunks, CHUNK, 2, 2, SSD_GROUPS, hpg).transpose(0, 4, 2, 3, 5, 1)
    gt = gt.reshape(batch * n_chunks, SSD_GROUPS, 4 * hpg, CHUNK)
    xb = SSD_INNER // gw
    bb = 2 * SSD_INNER // SSD_STATE
    cbk = bb + SSD_GROUPS
    cwb_off = SSD_INNER // SSD_STATE
    row = lambda width, off: pl.BlockSpec((seq, width), lambda r, g, off=off: (r, off + g))
    cw = lambda width, off: pl.BlockSpec((CONV_W, width), lambda r, g, off=off: (0, off + g))
    cb = lambda width, off: pl.BlockSpec((1, width), lambda r, g, off=off: (0, off + g))
    st_spec = pl.BlockSpec((None, 2, gw, SSD_STATE), lambda r, g: (r, 0, g, 0))
    conv_b2 = conv_b.reshape(1, -1)
    return pl.pallas_call(
        functools.partial(_ssd_kernel, n_chunks=n_chunks, unroll=2),
        grid=(batch, SSD_GROUPS),
        in_specs=[row(gw, 0), row(gw, xb), row(SSD_STATE, bb), row(SSD_STATE, cbk),
                  pl.BlockSpec((seq, LANES), lambda r, g: (r, 0)),
                  pl.BlockSpec((n_chunks, None, 4 * hpg, CHUNK), lambda r, g: (r, g, 0, 0)),
                  cw(gw, 0), cw(SSD_STATE, cwb_off), cw(SSD_STATE, cwb_off + SSD_GROUPS),
                  cb(gw, 0), cb(SSD_STATE, cwb_off), cb(SSD_STATE, cwb_off + SSD_GROUPS),
                  cb(gw, 0), cb(gw, 0), st_spec],
        out_specs=[pl.BlockSpec((seq, gw), lambda r, g: (r, g)), st_spec],
        out_shape=[jax.ShapeDtypeStruct((batch * seq, SSD_INNER), BF16),
                   jax.ShapeDtypeStruct(state0.shape, F32)],
        scratch_shapes=[pltpu.VMEM((seq, gw), F32), pltpu.VMEM((seq, SSD_STATE), BF16),
                        pltpu.VMEM((seq, SSD_STATE), BF16), pltpu.VMEM((seq, gw), F32),
                        pltpu.VMEM((seq, gw), F32),
                        pltpu.VMEM((CHUNK + 2 * HALO, gw), F32)],
        compiler_params=_params("arbitrary", "arbitrary"),
        name="ssd_scan",
    )(zx, zx, zx, zx, gates, gt, conv_w, conv_w, conv_w, conv_b2, conv_b2, conv_b2,
      dskip.reshape(1, -1), norm_w.reshape(1, -1), state0)


def _hi_lo(v):
    hi = v.astype(BF16)
    return hi, (v - hi.astype(F32)).astype(BF16)


def _lhs2(a):
    hi, lo = _hi_lo(a)
    return jnp.concatenate([hi, hi, lo], axis=1)


def _rhs2(b):
    hi, lo = _hi_lo(b)
    return jnp.concatenate([hi, lo, hi], axis=0)


def _gdn_gates_kernel(ab_ref, par_ref, o_ref):
    ab = ab_ref[...]
    pre, suf = _chunk_sums(-jnp.exp(par_ref[0:1, :]) * _softplus(ab + par_ref[1:2, :]))
    lane = lax.broadcasted_iota(jnp.int32, ab.shape, 1)
    o_ref[...] = jnp.where(lane < GDN_HEADS, pre, jnp.where(lane < 2 * GDN_HEADS, suf, jax.nn.sigmoid(ab)))


def _gdn_gates(ab_raw, par):
    t, w = ab_raw.shape
    rows = 4 * CHUNK
    return pl.pallas_call(
        _gdn_gates_kernel,
        grid=(t // rows,),
        in_specs=[pl.BlockSpec((rows, w), lambda i: (i, 0)), pl.BlockSpec((2, w), lambda i: (0, 0))],
        out_specs=pl.BlockSpec((rows, w), lambda i: (i, 0)),
        out_shape=jax.ShapeDtypeStruct((t, w), F32),
        compiler_params=_params("arbitrary"),
        name="gdn_gates",
    )(ab_raw, par)


def _gdn_kernel(q_ref, k_ref, v_ref, z_ref, gates_ref, gct_ref, cwq_ref, cwk_ref, cwv_ref, nw_ref, s0_ref,
                y_ref, sout_ref, qc_scr, kc_scr, vc_scr, sk_scr, sv_scr, qkd_scr, qe_scr, kwt_scr, el_scr,
                o_scr, win_scr, *, n_chunks, prep_unroll):
    h = pl.program_id(1)
    dk = GDN_DK

    def conv_body(c, carry):
        _conv_silu_chunk(q_ref, cwq_ref, 0.0, qc_scr, win_scr, c, n_chunks)
        _conv_silu_chunk(k_ref, cwk_ref, 0.0, kc_scr, win_scr, c, n_chunks)
        _conv_silu_chunk(v_ref, cwv_ref, 0.0, vc_scr, win_scr, c, n_chunks)
        r0 = pl.multiple_of(c * CHUNK, CHUNK)
        rows = pl.ds(r0, CHUNK)
        q = qc_scr[rows, :]
        qc_scr[rows, :] = q * lax.rsqrt(jnp.sum(q * q, axis=-1, keepdims=True) + EPS) * (dk ** -0.5)
        k = kc_scr[rows, :]
        kc_scr[rows, :] = k * lax.rsqrt(jnp.sum(k * k, axis=-1, keepdims=True) + EPS)
        return carry

    lax.fori_loop(0, n_chunks, conv_body, 0)

    ii = lax.broadcasted_iota(jnp.int32, (CHUNK, CHUNK), 0)
    jj = lax.broadcasted_iota(jnp.int32, (CHUNK, CHUNK), 1)
    eye = (ii == jj).astype(F32)
    glane = lax.broadcasted_iota(jnp.int32, (CHUNK, 4 * GDN_HEADS), 1)

    def prep(i, carry):
        chunks = []
        for u in range(prep_unroll):
            c = i * prep_unroll + u
            rows = pl.ds(pl.multiple_of(c * CHUNK, CHUNK), CHUNK)
            chunks.append(dict(c=c, rows=rows, q=qc_scr[rows, :], k=kc_scr[rows, :], v=vc_scr[rows, :],
                               gt=gates_ref[rows, :]))
        for ch in chunks:
            kb = ch["k"].astype(BF16)
            ch["kk"] = _dot_nt(kb, kb)
            ch["qk"] = _dot_nt(ch["q"].astype(BF16), kb)
        chains = []
        for ch in chunks:
            for d in range(2):
                idx = d * GDN_HEADS + h
                incl = (ii >= jj) if d == 0 else (ii <= jj)
                strict = (ii > jj) if d == 0 else (ii < jj)
                gt = ch["gt"]
                gc_col = jnp.sum(jnp.where(glane == idx, gt, 0.0), axis=1, keepdims=True)
                bt_col = jnp.sum(jnp.where(glane == 2 * GDN_HEADS + idx, gt, 0.0), axis=1, keepdims=True)
                gc_row = gct_ref[ch["c"], pl.ds(idx, 1), :]
                last = CHUNK - 1 if d == 0 else 0
                g_last = gc_col[last:last + 1, :]
                decay = jnp.where(incl, jnp.exp(gc_col - gc_row), 0.0)
                pw = -(jnp.where(strict, ch["kk"] * decay, 0.0) * bt_col)
                chains.append(dict(ch=ch, d=d, gc_col=gc_col, bt_col=bt_col, g_last=g_last, decay=decay,
                                   pw=pw, tinv=eye + pw))
        for _ in range(int(math.log2(CHUNK)) - 1):
            for cn in chains:
                cn["pw"] = _dot(_lhs2(cn["pw"]), _rhs2(cn["pw"]))
            for cn in chains:
                cn["tinv"] = cn["tinv"] + _dot(_lhs2(cn["tinv"]), _rhs2(cn["pw"]))
        for cn in chains:
            k, v = cn["ch"]["k"], cn["ch"]["v"]
            cn["egc"] = jnp.exp(cn["gc_col"])
            rhs = jnp.concatenate([k * (cn["bt_col"] * cn["egc"]), v * cn["bt_col"]], axis=1)
            cn["sol"] = _dot(_lhs2(cn["tinv"]), _rhs2(rhs))
        for cn in chains:
            ch, d, rows = cn["ch"], cn["d"], cn["ch"]["rows"]
            sk_scr[d, rows, :] = cn["sol"][:, :dk].astype(BF16)
            sv_scr[d, rows, :] = cn["sol"][:, dk:]
            qkd_scr[d, rows, :] = (ch["qk"] * cn["decay"]).astype(BF16)
            qe_scr[d, rows, :] = (ch["q"] * cn["egc"]).astype(BF16)
            kwt_scr[d, ch["c"]] = (ch["k"] * jnp.exp(cn["g_last"] - cn["gc_col"])).T.astype(BF16)
            el_scr[d, ch["c"]] = jnp.broadcast_to(jnp.exp(cn["g_last"]), (8, LANES))
        return carry

    lax.fori_loop(0, n_chunks // prep_unroll, prep, 0)

    def step(t, sts):
        cs = (t, n_chunks - 1 - t)
        rows = [pl.ds(pl.multiple_of(c * CHUNK, CHUNK), CHUNK) for c in cs]
        dirs = range(2)
        stb = [sts[d].astype(BF16) for d in dirs]
        ks = [_dot(sk_scr[d, rows[d], :], stb[d]) for d in dirs]
        oq = [_dot(qe_scr[d, rows[d], :], stb[d]) for d in dirs]
        ub = [(sv_scr[d, rows[d], :] - ks[d]).astype(BF16) for d in dirs]
        upd = [_dot(kwt_scr[d, cs[d]], ub[d]) for d in dirs]
        ou = [_dot(qkd_scr[d, rows[d], :], ub[d]) for d in dirs]
        for d in dirs:
            o_scr[d, rows[d], :] = oq[d] + ou[d]
        return tuple(sts[d] * el_scr[d, cs[d]][0:1, :] + upd[d] for d in dirs)

    sts = lax.fori_loop(0, n_chunks, step, (s0_ref[0], s0_ref[1]))

    for d in range(2):
        sout_ref[d] = sts[d]

    def fin_body(c, carry):
        r0 = pl.multiple_of(c * CHUNK, CHUNK)
        rows = pl.ds(r0, CHUNK)
        o = o_scr[0, rows, :] + o_scr[1, rows, :]
        o = o * lax.rsqrt(jnp.mean(o * o, axis=-1, keepdims=True) + EPS) * nw_ref[...]
        y_ref[rows, :] = (o * _silu(z_ref[rows, :].astype(F32))).astype(y_ref.dtype)
        return carry

    lax.fori_loop(0, n_chunks, fin_body, 0)


def _gdn_scan(qkvz, ab_raw, conv_w, par, norm_w, state0, *, batch, seq):
    n_chunks = seq // CHUNK
    hh = GDN_HEADS
    gates = _gdn_gates(ab_raw, par)
    gct = gates[:, :2 * hh].reshape(batch * n_chunks, CHUNK, 2 * hh).transpose(0, 2, 1)
    row = lambda off: pl.BlockSpec((seq, GDN_DK), lambda r, h, off=off: (r, off + h))
    cw = lambda off: pl.BlockSpec((CONV_W, GDN_DK), lambda r, h, off=off: (0, off + h))
    st_spec = pl.BlockSpec((None, 2, None, GDN_DK, GDN_DV), lambda r, h: (r, 0, h, 0, 0))
    return pl.pallas_call(
        functools.partial(_gdn_kernel, n_chunks=n_chunks, prep_unroll=4),
        grid=(batch, hh),
        in_specs=[row(0), row(hh), row(2 * hh), row(3 * hh),
                  pl.BlockSpec((seq, 4 * hh), lambda r, h: (r, 0)),
                  pl.BlockSpec((n_chunks, 2 * hh, CHUNK), lambda r, h: (r, 0, 0)),
                  cw(0), cw(hh), cw(2 * hh),
                  pl.BlockSpec((1, GDN_DV), lambda r, h: (0, 0)),
                  st_spec],
        out_specs=[pl.BlockSpec((seq, GDN_DV), lambda r, h: (r, h)), st_spec],
        out_shape=[jax.ShapeDtypeStruct((batch * seq, hh * GDN_DV), BF16),
                   jax.ShapeDtypeStruct(state0.shape, F32)],
        scratch_shapes=[pltpu.VMEM((seq, GDN_DK), F32), pltpu.VMEM((seq, GDN_DK), F32),
                        pltpu.VMEM((seq, GDN_DV), F32),
                        pltpu.VMEM((2, seq, GDN_DK), BF16), pltpu.VMEM((2, seq, GDN_DV), F32),
                        pltpu.VMEM((2, seq, CHUNK), BF16), pltpu.VMEM((2, seq, GDN_DK), BF16),
                        pltpu.VMEM((2, n_chunks, GDN_DK, CHUNK), BF16), pltpu.VMEM((2, n_chunks, 8, LANES), F32),
                        pltpu.VMEM((2, seq, GDN_DV), F32),
                        pltpu.VMEM((CHUNK + 2 * HALO, GDN_DK), F32)],
        compiler_params=_params("arbitrary", "arbitrary"),
        name="gdn_scan",
    )(qkvz, qkvz, qkvz, qkvz, gates, gct, conv_w, conv_w, conv_w, norm_w.reshape(1, -1), state0)


def _pad_cols(w, n):
    return jnp.pad(w, ((0, 0), (0, n - w.shape[1])))


def kernel(x_prompt, x_sample, cache_k, cache_v, state_ssd, state_delta, c, c_ctx, mod_w, mod_b, ln_g, ln_b, attn_w_in, attn_lam, attn_subln, attn_w_out, ssd_w_in, ssd_conv_w, ssd_conv_b, ssd_a_log, ssd_dt_bias, ssd_d, ssd_norm, ssd_w_out, gdn_w_in, gdn_conv_w, gdn_a_log, gdn_dt_bias, gdn_norm, gdn_w_out, moe_w_rg, moe_b_rg, moe_w_re, moe_b_re, moe_w_gate, moe_w_up, moe_w_down):
    nb_c, s_c, d = x_prompt.shape
    nb_l, s_l, _ = x_sample.shape
    past = cache_k.shape[2]
    sets = {"ctx": dict(batch=nb_c, seq=s_c, tm=min(512, s_c), tp=min(1024, nb_c * s_c)),
            "lat": dict(batch=nb_l, seq=s_l, tm=512, tp=min(1024, s_l))}
    xs = {"ctx": x_prompt.reshape(nb_c * s_c, d), "lat": x_sample.reshape(nb_l * s_l, d)}

    n_cond = 8 * ((1 + nb_l + 7) // 8)
    cond = jnp.zeros((n_cond, d), F32).at[0].set(c_ctx).at[1:1 + nb_l].set(c)
    mods_all = _modulations(cond, mod_w, mod_b).reshape(-1, n_cond, 6, d)
    rope = _rope_tables(s_l)

    new_k, new_v, new_ssd, new_gdn = [], [], [], []
    for i in range(DEPTH):
        kind, j = i % 3, i // 3
        mods = {"ctx": mods_all[i, 0:1], "lat": mods_all[i, 1:1 + nb_l]}
        mix = {}
        if kind == 0:
            lam_init = 0.8 - 0.6 * math.exp(-0.3 * i)
            w_in = attn_w_in[j].astype(BF16)
            w_out = attn_w_out[j].astype(BF16)
            for name, st in sets.items():
                lat = name == "lat"
                qkv = _in_proj(xs[name], mods[name], w_in, seq=st["seq"], out_dtype=BF16 if lat else F32,
                               tm=st["tp"], tn=1024, rope=rope if lat else None, n_rope_cols=2 * A_HEADS * A_VD)
                if lat:
                    ck = cache_k[:, j].reshape(nb_l, past, A_HEADS * A_VD)
                    cv = cache_v[:, j].reshape(nb_l, past, A_HEADS * A_VD)
                else:
                    ck = cv = None
                    hw = A_HEADS * A_VD
                    new_k.append(qkv[:, hw:2 * hw].reshape(nb_c, s_c, A_HEADS, 2, A_HD))
                    new_v.append(qkv[:, 2 * hw:].reshape(nb_c, s_c, A_HEADS, A_VD))
                a = _attention(qkv, ck, cv, attn_lam[j], attn_subln[j], batch=st["batch"], seq=st["seq"],
                               tq=min(512, st["seq"]), tk=256, lam_init=lam_init)
                mix[name] = (a, w_out)
        elif kind == 1:
            n_main = SSD_INNER + SSD_INNER + 2 * SSD_GROUPS * SSD_STATE
            w_main = ssd_w_in[j][:, :n_main].astype(BF16)
            w_dt = _pad_cols(ssd_w_in[j][:, n_main:], LANES).astype(BF16)
            w_out = ssd_w_out[j].astype(BF16)
            par = _pad_cols(jnp.stack([ssd_a_log[j].reshape(-1), ssd_dt_bias[j].reshape(-1)]), LANES)
            dskip = jnp.repeat(ssd_d[j], SSD_HD)
            for name, st in sets.items():
                zx = _in_proj(xs[name], mods[name], w_main, seq=st["seq"], out_dtype=BF16, tm=st["tp"], tn=1024)
                dt = _in_proj(xs[name], mods[name], w_dt, seq=st["seq"], out_dtype=F32, tm=st["tp"], tn=LANES)
                if name == "lat":
                    s0 = state_ssd[:, j].reshape(nb_l, 2, SSD_INNER, SSD_STATE)
                else:
                    s0 = jnp.zeros((nb_c, 2, SSD_INNER, SSD_STATE), F32)
                y, s_fin = _ssd_scan(zx, dt, ssd_conv_w[j], ssd_conv_b[j], par, dskip,
                                     ssd_norm[j], s0, batch=st["batch"], seq=st["seq"])
                if name == "ctx":
                    new_ssd.append(s_fin.reshape(nb_c, 2, SSD_HEADS, SSD_HD, SSD_STATE))
                mix[name] = (y, w_out)
        else:
            n_main = 3 * GDN_HEADS * GDN_DK + GDN_HEADS * GDN_DV
            w_main = gdn_w_in[j][:, :n_main].astype(BF16)
            w_ab = _pad_cols(gdn_w_in[j][:, n_main:], LANES).astype(BF16)
            w_out = gdn_w_out[j].astype(BF16)
            zeros = jnp.zeros((2 * GDN_HEADS,), F32)
            par = jnp.stack([jnp.concatenate([gdn_a_log[j].reshape(-1), zeros]),
                             jnp.concatenate([gdn_dt_bias[j].reshape(-1), zeros])])
            for name, st in sets.items():
                qkvz = _in_proj(xs[name], mods[name], w_main, seq=st["seq"], out_dtype=BF16, tm=st["tp"], tn=1024)
                ab = _in_proj(xs[name], mods[name], w_ab, seq=st["seq"], out_dtype=F32, tm=st["tp"], tn=LANES)
                if name == "lat":
                    s0 = state_delta[:, j]
                else:
                    s0 = jnp.zeros((nb_c, 2, GDN_HEADS, GDN_DK, GDN_DV), F32)
                y, s_fin = _gdn_scan(qkvz, ab[:, :4 * GDN_HEADS], gdn_conv_w[j], par, gdn_norm[j], s0,
                                     batch=st["batch"], seq=st["seq"])
                if name == "ctx":
                    new_gdn.append(s_fin)
                mix[name] = (y, w_out)

        w_router = _pad_cols(jnp.concatenate([moe_w_re[i], moe_w_rg[i]], axis=1), LANES)
        b_router = _pad_cols(jnp.concatenate([moe_b_re[i], moe_b_rg[i]])[None, :], LANES)
        w_gu = jnp.concatenate([moe_w_gate[i], moe_w_up[i]], axis=-1).astype(BF16)
        w_d = moe_w_down[i].astype(BF16)
        for name, st in sets.items():
            a, w_out = mix[name]
            x1 = _out_proj(a, xs[name], mods[name], w_out, ln_g[i, 0], ln_b[i, 0], seq=st["seq"], tm=st["tm"])
            xs[name] = _moe(x1, mods[name], w_router, b_router, w_gu, w_d, ln_g[i, 1], ln_b[i, 1],
                            seq=st["seq"], tm=st["tp"])

    return (xs["ctx"].reshape(nb_c, s_c, d), xs["lat"].reshape(nb_l, s_l, d),
            jnp.stack(new_k, axis=1), jnp.stack(new_v, axis=1),
            jnp.stack(new_ssd, axis=1), jnp.stack(new_gdn, axis=1))
```

```python
import functools
import math

import jax
import jax.numpy as jnp
from jax import lax
from jax.experimental import pallas as pl
from jax.experimental.pallas import tpu as pltpu

F32 = jnp.float32
BF16 = jnp.bfloat16
HIGHEST = lax.Precision.HIGHEST

D_MODEL = 1024
DEPTH = 4
GRID_W = 64
DN_ALPHA = (2.0 * DEPTH) ** 0.25
EPS = 1e-6
A_HEADS = 8
A_HD = 64
A_VD = 2 * A_HD
ROPE_BASE = 10000.0
SSD_INNER = 2 * D_MODEL
SSD_HD = 64
SSD_HEADS = SSD_INNER // SSD_HD
SSD_GROUPS = 8
SSD_STATE = 128
SSD_GROUP_W = SSD_INNER // SSD_GROUPS
GDN_HEADS = 8
GDN_DK = 128
GDN_DV = 128
CONV_W = 5
CHUNK = 64
MOE_GROUPS = 4
MOE_PER_GROUP = 4
MOE_EXPERTS = MOE_GROUPS * MOE_PER_GROUP
MOE_FF = 256
MOE_EXPERT_BLOCK = 4

LANES = 128
HALO = 16
VMEM_LIMIT = 52 * 1024 * 1024


def _params(*sem):
    return pltpu.CompilerParams(dimension_semantics=sem, vmem_limit_bytes=VMEM_LIMIT)


def _silu(v):
    return v * jax.nn.sigmoid(v)


def _softplus(v):
    return jnp.maximum(v, 0.0) + jnp.log(1.0 + jnp.exp(-jnp.abs(v)))


def _layer_norm(v, g, b):
    mu = jnp.mean(v, axis=-1, keepdims=True)
    d = v - mu
    var = jnp.mean(d * d, axis=-1, keepdims=True)
    return d * lax.rsqrt(var + EPS) * g + b


def _dot(a, b):
    return jnp.dot(a, b, preferred_element_type=F32)


def _dot_hi(a, b):
    return jnp.dot(a, b, preferred_element_type=F32, precision=HIGHEST)


def _dot_nt(a, b):
    return lax.dot_general(a, b, (((1,), (1,)), ((), ())), preferred_element_type=F32)


def _dot_tn(a, b):
    return lax.dot_general(a, b, (((0,), (0,)), ((), ())), preferred_element_type=F32)


def _mod_kernel(c_ref, w_ref, b_ref, o_ref):
    o_ref[...] = _dot_hi(_silu(c_ref[...]), w_ref[...]) + b_ref[...]


def _modulations(cond, mod_w, mod_b):
    depth, d, n = mod_w.shape
    r = cond.shape[0]
    tn = 1536
    return pl.pallas_call(
        _mod_kernel,
        grid=(depth, n // tn),
        in_specs=[pl.BlockSpec((r, d), lambda l, j: (0, 0)),
                  pl.BlockSpec((None, d, tn), lambda l, j: (l, 0, j)),
                  pl.BlockSpec((None, 1, tn), lambda l, j: (l, 0, j))],
        out_specs=pl.BlockSpec((None, r, tn), lambda l, j: (l, 0, j)),
        out_shape=jax.ShapeDtypeStruct((depth, r, n), F32),
        compiler_params=_params("arbitrary", "arbitrary"),
        name="modulation",
    )(cond, mod_w, mod_b.reshape(depth, 1, n))


def _in_proj_kernel(*refs, n_rope_tiles, tn):
    if n_rope_tiles:
        x_ref, mod_ref, w_ref, cos_ref, sa_ref, sb_ref, o_ref, h_scr = refs
    else:
        x_ref, mod_ref, w_ref, o_ref, h_scr = refs
    j = pl.program_id(1)

    @pl.when(j == 0)
    def _():
        h = x_ref[...] * (1.0 + mod_ref[1:2, :]) + mod_ref[0:1, :]
        h_scr[...] = h.astype(BF16)

    acc = _dot(h_scr[...], w_ref[...])
    if not n_rope_tiles:
        o_ref[...] = acc.astype(o_ref.dtype)
        return

    @pl.when(j < n_rope_tiles)
    def _():
        cos, sa, sb = cos_ref[...], sa_ref[...], sb_ref[...]
        for c in range(tn // LANES):
            a = acc[:, c * LANES:(c + 1) * LANES]
            r = a * cos + pltpu.roll(a, LANES - A_HD // 4, 1) * sa + pltpu.roll(a, A_HD // 4, 1) * sb
            o_ref[:, c * LANES:(c + 1) * LANES] = r.astype(o_ref.dtype)

    @pl.when(j >= n_rope_tiles)
    def _():
        o_ref[...] = acc.astype(o_ref.dtype)


def _in_proj(x, mods, w, *, seq, out_dtype, tm, tn, rope=None, n_rope_cols=0):
    t, d = x.shape
    n = w.shape[1]
    shared = mods.shape[0] == 1
    assert shared or seq % tm == 0
    tiles_per_req = max(seq // tm, 1)
    mod_map = (lambda i, j: (0, 0, 0)) if shared else (lambda i, j: (i // tiles_per_req, 0, 0))
    in_specs = [pl.BlockSpec((tm, d), lambda i, j: (i, 0)),
                pl.BlockSpec((None, 6, d), mod_map),
                pl.BlockSpec((d, tn), lambda i, j: (0, j))]
    args = [x, mods, w]
    n_rope_tiles = 0
    if rope is not None:
        n_rope_tiles = n_rope_cols // tn
        for tab in rope:
            in_specs.append(pl.BlockSpec((tm, LANES), lambda i, j: (i % tiles_per_req, 0)))
            args.append(tab)
    return pl.pallas_call(
        functools.partial(_in_proj_kernel, n_rope_tiles=n_rope_tiles, tn=tn),
        grid=(t // tm, n // tn),
        in_specs=in_specs,
        out_specs=pl.BlockSpec((tm, tn), lambda i, j: (i, j)),
        out_shape=jax.ShapeDtypeStruct((t, n), out_dtype),
        scratch_shapes=[pltpu.VMEM((tm, d), BF16)],
        compiler_params=_params("arbitrary", "arbitrary"),
        name="in_proj",
    )(*args)


def _out_proj_kernel(a_ref, x_ref, mod_ref, w_ref, g_ref, b_ref, o_ref):
    y = _dot(a_ref[...], w_ref[...])
    v = DN_ALPHA * x_ref[...] + mod_ref[2:3, :] * y
    o_ref[...] = _layer_norm(v, g_ref[...], b_ref[...])


def _out_proj(a, x, mods, w, ln_g, ln_b, *, seq, tm):
    t, d = x.shape
    k = a.shape[1]
    shared = mods.shape[0] == 1
    assert shared or seq % tm == 0
    tiles_per_req = max(seq // tm, 1)
    mod_map = (lambda i: (0, 0, 0)) if shared else (lambda i: (i // tiles_per_req, 0, 0))
    return pl.pallas_call(
        _out_proj_kernel,
        grid=(t // tm,),
        in_specs=[pl.BlockSpec((tm, k), lambda i: (i, 0)),
                  pl.BlockSpec((tm, d), lambda i: (i, 0)),
                  pl.BlockSpec((None, 6, d), mod_map),
                  pl.BlockSpec((k, d), lambda i: (0, 0)),
                  pl.BlockSpec((1, d), lambda i: (0, 0)),
                  pl.BlockSpec((1, d), lambda i: (0, 0))],
        out_specs=pl.BlockSpec((tm, d), lambda i: (i, 0)),
        out_shape=jax.ShapeDtypeStruct((t, d), F32),
        compiler_params=_params("arbitrary"),
        name="out_proj",
    )(a, x, mods, w, ln_g.reshape(1, d), ln_b.reshape(1, d))


def _moe_kernel(x_ref, mod_ref, wr_ref, br_ref, wgu_ref, wd_ref, g_ref, b_ref, o_ref,
                t_scr, gate_scr, acc_scr):
    e = pl.program_id(1)
    tm = x_ref.shape[0]
    lane = lax.broadcasted_iota(jnp.int32, (tm, LANES), 1)

    @pl.when(e == 0)
    def _():
        t = x_ref[...] * (1.0 + mod_ref[4:5, :]) + mod_ref[3:4, :]
        t_hi, t_lo = _hi_lo(t)
        t_scr[...] = t_hi
        w_hi, w_lo = _hi_lo(wr_ref[...])
        logits = _dot(t_hi, w_hi) + _dot(t_hi, w_lo) + _dot(t_lo, w_hi) + br_ref[...]
        lane_f = lane.astype(F32)
        neg = jnp.float32(-jnp.inf)
        big = jnp.float32(1e9)
        is_g = (lane >= MOE_EXPERTS) & (lane < MOE_EXPERTS + MOE_GROUPS)
        gl = jnp.where(is_g, logits, neg)
        gmax = jnp.max(gl, axis=-1, keepdims=True)
        pg_top = 1.0 / jnp.sum(jnp.exp(gl - gmax), axis=-1, keepdims=True)
        g_top = jnp.min(jnp.where(gl == gmax, lane_f, big), axis=-1, keepdims=True) - MOE_EXPERTS
        in_grp = (lane < MOE_EXPERTS) & ((lane // MOE_PER_GROUP).astype(F32) == g_top)
        el = jnp.where(in_grp, logits, neg)
        m1 = jnp.max(el, axis=-1, keepdims=True)
        i1 = jnp.min(jnp.where(el == m1, lane_f, big), axis=-1, keepdims=True)
        el2 = jnp.where(lane_f == i1, neg, el)
        m2 = jnp.max(el2, axis=-1, keepdims=True)
        i2 = jnp.min(jnp.where(el2 == m2, lane_f, big), axis=-1, keepdims=True)
        r = jnp.exp(m2 - m1)
        w1 = pg_top / (1.0 + r)
        w2 = pg_top * r / (1.0 + r)
        gate_scr[...] = jnp.where(lane_f == i1, w1, 0.0) + jnp.where(lane_f == i2, w2, 0.0)
        acc_scr[...] = jnp.zeros_like(acc_scr)

    gate = gate_scr[...]
    hids = []
    for k in range(MOE_EXPERT_BLOCK):
        hgu = _dot(t_scr[...], wgu_ref[k])
        g_e = jnp.sum(jnp.where(lane == e * MOE_EXPERT_BLOCK + k, gate, 0.0), axis=-1, keepdims=True)
        hids.append((_silu(hgu[:, :MOE_FF]) * hgu[:, MOE_FF:] * g_e).astype(BF16))
    acc_scr[...] += _dot(jnp.concatenate(hids, axis=1), wd_ref[...])

    @pl.when(e == MOE_EXPERTS // MOE_EXPERT_BLOCK - 1)
    def _():
        v = DN_ALPHA * x_ref[...] + mod_ref[5:6, :] * acc_scr[...]
        o_ref[...] = _layer_norm(v, g_ref[...], b_ref[...])


def _moe(x, mods, w_router, b_router, w_gu, w_d, ln_g, ln_b, *, seq, tm):
    t, d = x.shape
    shared = mods.shape[0] == 1
    assert shared or seq % tm == 0
    tiles_per_req = max(seq // tm, 1)
    mod_map = (lambda i, e: (0, 0, 0)) if shared else (lambda i, e: (i // tiles_per_req, 0, 0))
    return pl.pallas_call(
        _moe_kernel,
        grid=(t // tm, MOE_EXPERTS // MOE_EXPERT_BLOCK),
        in_specs=[pl.BlockSpec((tm, d), lambda i, e: (i, 0)),
                  pl.BlockSpec((None, 6, d), mod_map),
                  pl.BlockSpec((d, LANES), lambda i, e: (0, 0)),
                  pl.BlockSpec((1, LANES), lambda i, e: (0, 0)),
                  pl.BlockSpec((MOE_EXPERT_BLOCK, d, 2 * MOE_FF), lambda i, e: (e, 0, 0)),
                  pl.BlockSpec((None, MOE_EXPERT_BLOCK * MOE_FF, d), lambda i, e: (e, 0, 0)),
                  pl.BlockSpec((1, d), lambda i, e: (0, 0)),
                  pl.BlockSpec((1, d), lambda i, e: (0, 0))],
        out_specs=pl.BlockSpec((tm, d), lambda i, e: (i, 0)),
        out_shape=jax.ShapeDtypeStruct((t, d), F32),
        scratch_shapes=[pltpu.VMEM((tm, d), BF16), pltpu.VMEM((tm, LANES), F32), pltpu.VMEM((tm, d), F32)],
        compiler_params=_params("arbitrary", "arbitrary"),
        name="moe",
    )(x, mods, w_router, b_router, w_gu,
      w_d.reshape(MOE_EXPERTS // MOE_EXPERT_BLOCK, MOE_EXPERT_BLOCK * MOE_FF, d),
      ln_g.reshape(1, d), ln_b.reshape(1, d))


def _attn_kernel(*refs, n_seq_tiles, tk, tw, has_cache, lam_init):
    if has_cache:
        q_ref, k_ref, v_ref, ck_ref, cv_ref, lam_ref, sub_ref, o_ref, kt_scr, vb_scr, s_scr, acc_scr = refs
    else:
        q_ref, k_ref, v_ref, lam_ref, sub_ref, o_ref, kt_scr, vb_scr, s_scr, acc_scr = refs
    tq = q_ref.shape[0]
    off = tk if has_cache else 0
    n_keys = n_seq_tiles * tk + off

    @pl.when(pl.program_id(2) == 0)
    def _():
        if has_cache:
            kt_scr[:, 0:tk] = ck_ref[...].T.astype(BF16)
            vb_scr[0:tk, :] = cv_ref[...].astype(BF16)
        for i in range(n_seq_tiles):
            kt_scr[:, off + i * tk:off + (i + 1) * tk] = k_ref[i * tk:(i + 1) * tk, :].astype(F32).T.astype(BF16)
            vb_scr[off + i * tk:off + (i + 1) * tk, :] = v_ref[i * tk:(i + 1) * tk, :].astype(BF16)

    lane = lax.broadcasted_iota(jnp.int32, (tq, LANES), 1)
    q = q_ref[...].astype(F32) * (A_HD ** -0.5)
    qs = (jnp.where(lane < A_HD, q, 0.0).astype(BF16), jnp.where(lane >= A_HD, q, 0.0).astype(BF16))

    tiles = [(a, min(a + tw, n_keys)) for a in range(0, n_keys, tw)]
    ms = [jnp.full((tq, 1), -jnp.inf, F32)] * 2
    for a, b in tiles:
        k_tile = kt_scr[:, a:b]
        for c in range(2):
            s = _dot(qs[c], k_tile)
            s_scr[c, :, a:b] = s
            ms[c] = jnp.maximum(ms[c], jnp.max(s, axis=-1, keepdims=True))

    ls = [jnp.zeros((tq, 1), F32)] * 2
    for i, (a, b) in enumerate(tiles):
        v_tile = vb_scr[a:b, :]
        for c in range(2):
            p = jnp.exp(s_scr[c, :, a:b] - ms[c])
            ls[c] = ls[c] + jnp.sum(p, axis=-1, keepdims=True)
            pv = _dot(p.astype(BF16), v_tile)
            if i == 0:
                acc_scr[c] = pv
            else:
                acc_scr[c] += pv

    lp = lam_ref[...]
    lam = (jnp.exp(jnp.sum(lp[0:1, :] * lp[1:2, :], axis=-1, keepdims=True))
           - jnp.exp(jnp.sum(lp[2:3, :] * lp[3:4, :], axis=-1, keepdims=True)) + lam_init)
    o = acc_scr[0] / ls[0] - lam * (acc_scr[1] / ls[1])
    o = o * lax.rsqrt(jnp.mean(o * o, axis=-1, keepdims=True) + EPS) * sub_ref[...]
    o_ref[...] = (o * (1.0 - lam_init)).astype(o_ref.dtype)


def _attention(qkv, cache_k, cache_v, lam_p, subln, *, batch, seq, tq, tk, lam_init):
    has_cache = cache_k is not None
    hw = A_HEADS * A_VD
    nq = seq // tq
    n_seq_tiles = seq // tk
    in_specs = [pl.BlockSpec((tq, A_VD), lambda b, h, i: (b * nq + i, h)),
                pl.BlockSpec((seq, A_VD), lambda b, h, i: (b, A_HEADS + h)),
                pl.BlockSpec((seq, A_VD), lambda b, h, i: (b, 2 * A_HEADS + h))]
    args = [qkv, qkv, qkv]
    n_tiles = n_seq_tiles
    if has_cache:
        past = cache_k.shape[1]
        assert past == tk
        in_specs += [pl.BlockSpec((None, past, A_VD), lambda b, h, i: (b, 0, h)),
                     pl.BlockSpec((None, past, A_VD), lambda b, h, i: (b, 0, h))]
        args += [cache_k, cache_v]
        n_tiles += 1
    in_specs += [pl.BlockSpec((4, A_HD), lambda b, h, i: (0, 0)),
                 pl.BlockSpec((1, A_VD), lambda b, h, i: (0, 0))]
    args += [lam_p, subln.reshape(1, A_VD)]
    return pl.pallas_call(
        functools.partial(_attn_kernel, n_seq_tiles=n_seq_tiles, tk=tk, tw=tk, has_cache=has_cache,
                          lam_init=lam_init),
        grid=(batch, A_HEADS, nq),
        in_specs=in_specs,
        out_specs=pl.BlockSpec((tq, A_VD), lambda b, h, i: (b * nq + i, h)),
        out_shape=jax.ShapeDtypeStruct((batch * seq, hw), BF16),
        scratch_shapes=[pltpu.VMEM((A_VD, n_tiles * tk), BF16), pltpu.VMEM((n_tiles * tk, A_VD), BF16),
                        pltpu.VMEM((2, tq, n_tiles * tk), F32), pltpu.VMEM((2, tq, A_VD), F32)],
        compiler_params=_params("arbitrary", "arbitrary", "arbitrary"),
        name="diff_attention",
    )(*args)


def _rope_tables(seq):
    quarter = A_HD // 4
    inv_freq = ROPE_BASE ** (-jnp.arange(quarter, dtype=F32) / quarter)
    pos = jnp.arange(seq)
    row = (pos // GRID_W).astype(F32)
    col = (pos % GRID_W).astype(F32)
    lane = jnp.arange(LANES)
    d = lane % A_HD
    p = jnp.where((d < A_HD // 2)[None, :], row[:, None], col[:, None])
    ang = p * inv_freq[d % quarter][None, :]
    first = ((d % (A_HD // 2)) < quarter)[None, :]
    cos, sin = jnp.cos(ang), jnp.sin(ang)
    return cos, jnp.where(first, -sin, 0.0), jnp.where(first, 0.0, sin)


def _conv_silu_chunk(raw_ref, w_ref, bias, dst_ref, win_scr, c, n_chunks):
    width = raw_ref.shape[1]
    seq = n_chunks * CHUNK
    r0 = pl.multiple_of(c * CHUNK, CHUNK)
    prev = raw_ref[pl.ds(pl.multiple_of(jnp.maximum(r0 - HALO, 0), HALO), HALO), :].astype(F32)
    nxt = raw_ref[pl.ds(pl.multiple_of(jnp.minimum(r0 + CHUNK, seq - HALO), HALO), HALO), :].astype(F32)
    win_scr[0:HALO, 0:width] = jnp.where(c > 0, prev, 0.0)
    win_scr[HALO:HALO + CHUNK, 0:width] = raw_ref[pl.ds(r0, CHUNK), :].astype(F32)
    win_scr[HALO + CHUNK:, 0:width] = jnp.where(c < n_chunks - 1, nxt, 0.0)
    acc = jnp.zeros((CHUNK, width), F32) + bias
    for k in range(CONV_W):
        s = HALO - CONV_W // 2 + k
        acc = acc + w_ref[k:k + 1, :] * win_scr[s:s + CHUNK, 0:width]
    dst_ref[pl.ds(r0, CHUNK), :] = _silu(acc).astype(dst_ref.dtype)


def _tri(n, upper):
    i = lax.broadcasted_iota(jnp.int32, (n, n), 0)
    j = lax.broadcasted_iota(jnp.int32, (n, n), 1)
    return ((i <= j) if upper else (i >= j)).astype(F32)


def _split3(v):
    hi = v.astype(BF16)
    r1 = v - hi.astype(F32)
    mid = r1.astype(BF16)
    lo = (r1 - mid.astype(F32)).astype(BF16)
    return hi, mid, lo


def _chunk_sums(v):
    rows = v.shape[0]
    i = lax.broadcasted_iota(jnp.int32, (rows, rows), 0)
    j = lax.broadcasted_iota(jnp.int32, (rows, rows), 1)
    same = (i // CHUNK) == (j // CHUNK)
    lower = (same & (i >= j)).astype(BF16)
    upper = (same & (i <= j)).astype(BF16)
    pieces = _split3(v)
    return sum(_dot(lower, p) for p in pieces), sum(_dot(upper, p) for p in pieces)


def _ssd_gates_kernel(dt_ref, par_ref, o_ref):
    nh = 2 * SSD_HEADS
    dt = _softplus(dt_ref[...] + par_ref[1:2, :])
    pre, suf = _chunk_sums(dt * -jnp.exp(par_ref[0:1, :]))
    lane = lax.broadcasted_iota(jnp.int32, dt.shape, 1)
    cs = pltpu.roll(jnp.where(lane < SSD_HEADS, pre, suf), nh, 1)
    o_ref[...] = jnp.where(lane < nh, dt, cs)


def _ssd_gates(dt_raw, par):
    t, w = dt_raw.shape
    rows = 4 * CHUNK
    return pl.pallas_call(
        _ssd_gates_kernel,
        grid=(t // rows,),
        in_specs=[pl.BlockSpec((rows, w), lambda i: (i, 0)), pl.BlockSpec((2, w), lambda i: (0, 0))],
        out_specs=pl.BlockSpec((rows, w), lambda i: (i, 0)),
        out_shape=jax.ShapeDtypeStruct((t, w), F32),
        compiler_params=_params("arbitrary"),
        name="ssd_gates",
    )(dt_raw, par)


def _ssd_kernel(z_ref, x_ref, b_ref, c_ref, gates_ref, gt_ref, cwx_ref, cwb_ref, cwc_ref, cbx_ref, cbb_ref, cbc_ref,
                dskip_ref, nw_ref, s0_ref, y_ref, sout_ref,
                xc_scr, bc_scr, cc_scr, yf_scr, yb_scr, win_scr, *, n_chunks, unroll):
    g = pl.program_id(1)
    gw = SSD_GROUP_W
    hpg = SSD_HEADS // SSD_GROUPS
    nh = 2 * SSD_HEADS

    def conv_body(c, carry):
        _conv_silu_chunk(x_ref, cwx_ref, cbx_ref[...], xc_scr, win_scr, c, n_chunks)
        _conv_silu_chunk(b_ref, cwb_ref, cbb_ref[...], bc_scr, win_scr, c, n_chunks)
        _conv_silu_chunk(c_ref, cwc_ref, cbc_ref[...], cc_scr, win_scr, c, n_chunks)
        return carry

    lax.fori_loop(0, n_chunks, conv_body, 0)

    ii = lax.broadcasted_iota(jnp.int32, (CHUNK, gw), 0)
    jj = lax.broadcasted_iota(jnp.int32, (CHUNK, gw), 1) % CHUNK
    masks = ((ii >= jj), (ii <= jj))
    bi = lax.broadcasted_iota(jnp.int32, (gw, gw), 0)
    bj = lax.broadcasted_iota(jnp.int32, (gw, gw), 1)
    blockdiag = (bi // SSD_HD) == (bj // SSD_HD)
    sr = lax.broadcasted_iota(jnp.int32, (2 * nh, 2 * gw), 0)
    sl = lax.broadcasted_iota(jnp.int32, (2 * nh, 2 * gw), 1)
    sels = [(sr == (sl // gw) * nh + d * SSD_HEADS + g * hpg + (sl % gw) // SSD_HD).astype(BF16) for d in range(2)]
    sels = [jnp.concatenate([s, s], axis=0) for s in sels]

    def step(t, sts):
        chains = []
        for k in range(unroll):
            for d in range(2):
                c = t * unroll + k if d == 0 else n_chunks - 1 - (t * unroll + k)
                rows = pl.ds(pl.multiple_of(c * CHUNK, CHUNK), CHUNK)
                base = d * hpg
                dt_r = jnp.concatenate([gt_ref[c, base + hh:base + hh + 1, :] for hh in range(hpg)], axis=1)
                cs_r = jnp.concatenate([gt_ref[c, 2 * hpg + base + hh:2 * hpg + base + hh + 1, :]
                                        for hh in range(hpg)], axis=1)
                chains.append(dict(d=d, rows=rows, gt=gates_ref[rows, :], dt_r=dt_r, cs_r=cs_r,
                                   xq=xc_scr[rows, :], bq=bc_scr[rows, :], cq=cc_scr[rows, :]))
        for cn in chains:
            e = _dot(jnp.concatenate(_hi_lo(cn["gt"]), axis=1), sels[cn["d"]])
            cn["dt_e"], cn["cs_e"] = e[:, :gw], e[:, gw:]
            cn["cb"] = _dot_nt(cn["cq"], jnp.concatenate([cn["bq"]] * hpg, axis=0))
            cn["bt"] = cn["bq"].astype(F32).T.astype(BF16)
        for cn in chains:
            d, cs_e = cn["d"], cn["cs_e"]
            last = CHUNK - 1 if d == 0 else 0
            cs_last = cs_e[last:last + 1, :]
            m = jnp.where(masks[d], jnp.exp(cs_e - cn["cs_r"]), 0.0) * cn["dt_r"] * cn["cb"]
            xbd = jnp.concatenate([cn["xq"].astype(BF16)] * hpg, axis=0)
            xbd = jnp.where(blockdiag, xbd, jnp.zeros_like(xbd))
            cn["y"] = _dot(m.astype(BF16), xbd)
            xw = (cn["xq"] * (jnp.exp(cs_last - cs_e) * cn["dt_e"])).astype(BF16)
            cn["upd"] = _dot(cn["bt"], xw)
            cn["dec"] = jnp.exp(cs_last)
            cn["ecs"] = jnp.exp(cs_e)
        sts = list(sts)
        for cn in chains:
            d = cn["d"]
            y = cn["y"] + _dot(cn["cq"], sts[d].astype(BF16)) * cn["ecs"]
            sts[d] = sts[d] * cn["dec"] + cn["upd"]
            if d == 0:
                yf_scr[cn["rows"], :] = y
            else:
                yb_scr[cn["rows"], :] = y
        return tuple(sts)

    sts = lax.fori_loop(0, n_chunks // unroll, step, (s0_ref[0].T, s0_ref[1].T))

    for d in range(2):
        sout_ref[d] = sts[d].T

    fin_rows = 4 * CHUNK

    def fin_body(c, carry):
        rows = pl.ds(pl.multiple_of(c * fin_rows, fin_rows), fin_rows)
        y = (yf_scr[rows, :] + yb_scr[rows, :]) + dskip_ref[...] * xc_scr[rows, :]
        y = y * _silu(z_ref[rows, :].astype(F32))
        y = y * lax.rsqrt(jnp.mean(y * y, axis=-1, keepdims=True) + EPS) * nw_ref[...]
        y_ref[rows, :] = y.astype(y_ref.dtype)
        return carry

    lax.fori_loop(0, n_chunks * CHUNK // fin_rows, fin_body, 0)


def _ssd_scan(zx, dt_raw, conv_w, conv_b, par, dskip, norm_w, state0, *, batch, seq):
    gw = SSD_GROUP_W
    hpg = SSD_HEADS // SSD_GROUPS
    n_chunks = seq // CHUNK
    gates = _ssd_gates(dt_raw, par)
    gt = gates.reshape(batch * n_chunks, CHUNK, 2, 2, SSD_GROUPS, hpg).transpose(0, 4, 2, 3, 5, 1)
    gt = gt.reshape(batch * n_chunks, SSD_GROUPS, 4 * hpg, CHUNK)
    xb = SSD_INNER // gw
    bb = 2 * SSD_INNER // SSD_STATE
    cbk = bb + SSD_GROUPS
    cwb_off = SSD_INNER // SSD_STATE
    row = lambda width, off: pl.BlockSpec((seq, width), lambda r, g, off=off: (r, off + g))
    cw = lambda width, off: pl.BlockSpec((CONV_W, width), lambda r, g, off=off: (0, off + g))
    cb = lambda width, off: pl.BlockSpec((1, width), lambda r, g, off=off: (0, off + g))
    st_spec = pl.BlockSpec((None, 2, gw, SSD_STATE), lambda r, g: (r, 0, g, 0))
    conv_b2 = conv_b.reshape(1, -1)
    return pl.pallas_call(
        functools.partial(_ssd_kernel, n_chunks=n_chunks, unroll=2),
        grid=(batch, SSD_GROUPS),
        in_specs=[row(gw, 0), row(gw, xb), row(SSD_STATE, bb), row(SSD_STATE, cbk),
                  pl.BlockSpec((seq, LANES), lambda r, g: (r, 0)),
                  pl.BlockSpec((n_chunks, None, 4 * hpg, CHUNK), lambda r, g: (r, g, 0, 0)),
                  cw(gw, 0), cw(SSD_STATE, cwb_off), cw(SSD_STATE, cwb_off + SSD_GROUPS),
                  cb(gw, 0), cb(SSD_STATE, cwb_off), cb(SSD_STATE, cwb_off + SSD_GROUPS),
                  cb(gw, 0), cb(gw, 0), st_spec],
        out_specs=[pl.BlockSpec((seq, gw), lambda r, g: (r, g)), st_spec],
        out_shape=[jax.ShapeDtypeStruct((batch * seq, SSD_INNER), BF16),
                   jax.ShapeDtypeStruct(state0.shape, F32)],
        scratch_shapes=[pltpu.VMEM((seq, gw), F32), pltpu.VMEM((seq, SSD_STATE), BF16),
                        pltpu.VMEM((seq, SSD_STATE), BF16), pltpu.VMEM((seq, gw), F32),
                        pltpu.VMEM((seq, gw), F32),
                        pltpu.VMEM((CHUNK + 2 * HALO, gw), F32)],
        compiler_params=_params("arbitrary", "arbitrary"),
        name="ssd_scan",
    )(zx, zx, zx, zx, gates, gt, conv_w, conv_w, conv_w, conv_b2, conv_b2, conv_b2,
      dskip.reshape(1, -1), norm_w.reshape(1, -1), state0)


def _hi_lo(v):
    hi = v.astype(BF16)
    return hi, (v - hi.astype(F32)).astype(BF16)


def _lhs2(a):
    hi, lo = _hi_lo(a)
    return jnp.concatenate([hi, hi, lo], axis=1)


def _rhs2(b):
    hi, lo = _hi_lo(b)
    return jnp.concatenate([hi, lo, hi], axis=0)


def _gdn_gates_kernel(ab_ref, par_ref, o_ref):
    ab = ab_ref[...]
    pre, suf = _chunk_sums(-jnp.exp(par_ref[0:1, :]) * _softplus(ab + par_ref[1:2, :]))
    lane = lax.broadcasted_iota(jnp.int32, ab.shape, 1)
    o_ref[...] = jnp.where(lane < GDN_HEADS, pre, jnp.where(lane < 2 * GDN_HEADS, suf, jax.nn.sigmoid(ab)))


def _gdn_gates(ab_raw, par):
    t, w = ab_raw.shape
    rows = 4 * CHUNK
    return pl.pallas_call(
        _gdn_gates_kernel,
        grid=(t // rows,),
        in_specs=[pl.BlockSpec((rows, w), lambda i: (i, 0)), pl.BlockSpec((2, w), lambda i: (0, 0))],
        out_specs=pl.BlockSpec((rows, w), lambda i: (i, 0)),
        out_shape=jax.ShapeDtypeStruct((t, w), F32),
        compiler_params=_params("arbitrary"),
        name="gdn_gates",
    )(ab_raw, par)


def _gdn_kernel(q_ref, k_ref, v_ref, z_ref, gates_ref, gct_ref, cwq_ref, cwk_ref, cwv_ref, nw_ref, s0_ref,
                y_ref, sout_ref, qc_scr, kc_scr, vc_scr, w_scr, n_scr, qe_scr, el_scr,
                o_scr, win_scr, *, n_chunks, prep_unroll):
    h = pl.program_id(1)
    dk = GDN_DK

    def conv_body(c, carry):
        _conv_silu_chunk(q_ref, cwq_ref, 0.0, qc_scr, win_scr, c, n_chunks)
        _conv_silu_chunk(k_ref, cwk_ref, 0.0, kc_scr, win_scr, c, n_chunks)
        _conv_silu_chunk(v_ref, cwv_ref, 0.0, vc_scr, win_scr, c, n_chunks)
        r0 = pl.multiple_of(c * CHUNK, CHUNK)
        rows = pl.ds(r0, CHUNK)
        q = qc_scr[rows, :]
        qc_scr[rows, :] = q * lax.rsqrt(jnp.sum(q * q, axis=-1, keepdims=True) + EPS) * (dk ** -0.5)
        k = kc_scr[rows, :]
        kc_scr[rows, :] = k * lax.rsqrt(jnp.sum(k * k, axis=-1, keepdims=True) + EPS)
        return carry

    lax.fori_loop(0, n_chunks, conv_body, 0)

    ii = lax.broadcasted_iota(jnp.int32, (CHUNK, CHUNK), 0)
    jj = lax.broadcasted_iota(jnp.int32, (CHUNK, CHUNK), 1)
    eye = (ii == jj).astype(F32)
    glane = lax.broadcasted_iota(jnp.int32, (CHUNK, 4 * GDN_HEADS), 1)

    def prep(i, carry):
        chunks = []
        for u in range(prep_unroll):
            c = i * prep_unroll + u
            rows = pl.ds(pl.multiple_of(c * CHUNK, CHUNK), CHUNK)
            chunks.append(dict(c=c, rows=rows, q=qc_scr[rows, :], k=kc_scr[rows, :], v=vc_scr[rows, :],
                               gt=gates_ref[rows, :]))
        for ch in chunks:
            kb = ch["k"].astype(BF16)
            ch["kk"] = _dot_nt(kb, kb)
            ch["qk"] = _dot_nt(ch["q"].astype(BF16), kb)
        chains = []
        for ch in chunks:
            for d in range(2):
                idx = d * GDN_HEADS + h
                incl = (ii >= jj) if d == 0 else (ii <= jj)
                strict = (ii > jj) if d == 0 else (ii < jj)
                gt = ch["gt"]
                gc_col = jnp.sum(jnp.where(glane == idx, gt, 0.0), axis=1, keepdims=True)
                bt_col = jnp.sum(jnp.where(glane == 2 * GDN_HEADS + idx, gt, 0.0), axis=1, keepdims=True)
                gc_row = gct_ref[ch["c"], pl.ds(idx, 1), :]
                last = CHUNK - 1 if d == 0 else 0
                g_last = gc_col[last:last + 1, :]
                decay = jnp.where(incl, jnp.exp(gc_col - gc_row), 0.0)
                pw = -(jnp.where(strict, ch["kk"] * decay, 0.0) * bt_col)
                chains.append(dict(ch=ch, d=d, gc_col=gc_col, bt_col=bt_col, g_last=g_last, decay=decay,
                                   pw=pw, tinv=eye + pw))
        for _ in range(int(math.log2(CHUNK)) - 1):
            for cn in chains:
                cn["pw"] = _dot(_lhs2(cn["pw"]), _rhs2(cn["pw"]))
            for cn in chains:
                cn["tinv"] = cn["tinv"] + _dot(_lhs2(cn["tinv"]), _rhs2(cn["pw"]))
        for cn in chains:
            k, v = cn["ch"]["k"], cn["ch"]["v"]
            cn["egc"] = jnp.exp(cn["gc_col"])
            rhs = jnp.concatenate([k * (cn["bt_col"] * cn["egc"]), v * cn["bt_col"]], axis=1)
            cn["sol"] = _dot(_lhs2(cn["tinv"]), _rhs2(rhs))
        for cn in chains:
            ch = cn["ch"]
            kwt = (ch["k"] * jnp.exp(cn["g_last"] - cn["gc_col"])).T.astype(BF16)
            qkd = (ch["qk"] * cn["decay"]).astype(BF16)
            cn["prod"] = _dot(jnp.concatenate([kwt, qkd], axis=0), cn["sol"].astype(BF16))
        for cn in chains:
            ch, d, rows, prod = cn["ch"], cn["d"], cn["ch"]["rows"], cn["prod"]
            w_scr[d, ch["c"]] = prod[:dk, :dk].astype(BF16)
            n_scr[d, ch["c"]] = prod[:dk, dk:]
            qe_scr[d, rows, :] = (ch["q"] * cn["egc"] - prod[dk:, :dk]).astype(BF16)
            o_scr[d, rows, :] = prod[dk:, dk:]
            el_scr[d, ch["c"]] = jnp.broadcast_to(jnp.exp(cn["g_last"]), (8, LANES))
        return carry

    lax.fori_loop(0, n_chunks // prep_unroll, prep, 0)

    def step(t, sts):
        cs = (t, n_chunks - 1 - t)
        rows = [pl.ds(pl.multiple_of(c * CHUNK, CHUNK), CHUNK) for c in cs]
        dirs = range(2)
        stb = [sts[d].astype(BF16) for d in dirs]
        ws = [_dot(w_scr[d, cs[d]], stb[d]) for d in dirs]
        oq = [_dot(qe_scr[d, rows[d], :], stb[d]) for d in dirs]
        for d in dirs:
            o_scr[d, rows[d], :] += oq[d]
        return tuple(sts[d] * el_scr[d, cs[d]][0:1, :] - ws[d] + n_scr[d, cs[d]] for d in dirs)

    sts = lax.fori_loop(0, n_chunks, step, (s0_ref[0], s0_ref[1]))

    for d in range(2):
        sout_ref[d] = sts[d]

    fin_rows = 4 * CHUNK

    def fin_body(c, carry):
        rows = pl.ds(pl.multiple_of(c * fin_rows, fin_rows), fin_rows)
        o = o_scr[0, rows, :] + o_scr[1, rows, :]
        o = o * lax.rsqrt(jnp.mean(o * o, axis=-1, keepdims=True) + EPS) * nw_ref[...]
        y_ref[rows, :] = (o * _silu(z_ref[rows, :].astype(F32))).astype(y_ref.dtype)
        return carry

    lax.fori_loop(0, n_chunks * CHUNK // fin_rows, fin_body, 0)


def _gdn_scan(qkvz, ab_raw, conv_w, par, norm_w, state0, *, batch, seq):
    n_chunks = seq // CHUNK
    hh = GDN_HEADS
    gates = _gdn_gates(ab_raw, par)
    gct = gates[:, :2 * hh].reshape(batch * n_chunks, CHUNK, 2 * hh).transpose(0, 2, 1)
    row = lambda off: pl.BlockSpec((seq, GDN_DK), lambda r, h, off=off: (r, off + h))
    cw = lambda off: pl.BlockSpec((CONV_W, GDN_DK), lambda r, h, off=off: (0, off + h))
    st_spec = pl.BlockSpec((None, 2, None, GDN_DK, GDN_DV), lambda r, h: (r, 0, h, 0, 0))
    return pl.pallas_call(
        functools.partial(_gdn_kernel, n_chunks=n_chunks, prep_unroll=4),
        grid=(batch, hh),
        in_specs=[row(0), row(hh), row(2 * hh), row(3 * hh),
                  pl.BlockSpec((seq, 4 * hh), lambda r, h: (r, 0)),
                  pl.BlockSpec((n_chunks, 2 * hh, CHUNK), lambda r, h: (r, 0, 0)),
                  cw(0), cw(hh), cw(2 * hh),
                  pl.BlockSpec((1, GDN_DV), lambda r, h: (0, 0)),
                  st_spec],
        out_specs=[pl.BlockSpec((seq, GDN_DV), lambda r, h: (r, h)), st_spec],
        out_shape=[jax.ShapeDtypeStruct((batch * seq, hh * GDN_DV), BF16),
                   jax.ShapeDtypeStruct(state0.shape, F32)],
        scratch_shapes=[pltpu.VMEM((seq, GDN_DK), F32), pltpu.VMEM((seq, GDN_DK), F32),
                        pltpu.VMEM((seq, GDN_DV), F32),
                        pltpu.VMEM((2, n_chunks, GDN_DK, GDN_DK), BF16), pltpu.VMEM((2, n_chunks, GDN_DK, GDN_DV), F32),
                        pltpu.VMEM((2, seq, GDN_DK), BF16), pltpu.VMEM((2, n_chunks, 8, LANES), F32),
                        pltpu.VMEM((2, seq, GDN_DV), F32),
                        pltpu.VMEM((CHUNK + 2 * HALO, GDN_DK), F32)],
        compiler_params=_params("arbitrary", "arbitrary"),
        name="gdn_scan",
    )(qkvz, qkvz, qkvz, qkvz, gates, gct, conv_w, conv_w, conv_w, norm_w.reshape(1, -1), state0)


def _pad_cols(w, n):
    return jnp.pad(w, ((0, 0), (0, n - w.shape[1])))


def kernel(x_prompt, x_sample, cache_k, cache_v, state_ssd, state_delta, c, c_ctx, mod_w, mod_b, ln_g, ln_b, attn_w_in, attn_lam, attn_subln, attn_w_out, ssd_w_in, ssd_conv_w, ssd_conv_b, ssd_a_log, ssd_dt_bias, ssd_d, ssd_norm, ssd_w_out, gdn_w_in, gdn_conv_w, gdn_a_log, gdn_dt_bias, gdn_norm, gdn_w_out, moe_w_rg, moe_b_rg, moe_w_re, moe_b_re, moe_w_gate, moe_w_up, moe_w_down):
    nb_c, s_c, d = x_prompt.shape
    nb_l, s_l, _ = x_sample.shape
    past = cache_k.shape[2]
    sets = {"ctx": dict(batch=nb_c, seq=s_c, tm=min(512, s_c), tp=min(1024, nb_c * s_c)),
            "lat": dict(batch=nb_l, seq=s_l, tm=512, tp=min(1024, s_l))}
    xs = {"ctx": x_prompt.reshape(nb_c * s_c, d), "lat": x_sample.reshape(nb_l * s_l, d)}

    n_cond = 8 * ((1 + nb_l + 7) // 8)
    cond = jnp.zeros((n_cond, d), F32).at[0].set(c_ctx).at[1:1 + nb_l].set(c)
    mods_all = _modulations(cond, mod_w, mod_b).reshape(-1, n_cond, 6, d)
    rope = _rope_tables(s_l)

    new_k, new_v, new_ssd, new_gdn = [], [], [], []
    for i in range(DEPTH):
        kind, j = i % 3, i // 3
        mods = {"ctx": mods_all[i, 0:1], "lat": mods_all[i, 1:1 + nb_l]}
        mix = {}
        if kind == 0:
            lam_init = 0.8 - 0.6 * math.exp(-0.3 * i)
            w_in = attn_w_in[j].astype(BF16)
            w_out = attn_w_out[j].astype(BF16)
            for name, st in sets.items():
                lat = name == "lat"
                qkv = _in_proj(xs[name], mods[name], w_in, seq=st["seq"], out_dtype=BF16 if lat else F32,
                               tm=st["tp"], tn=1024, rope=rope if lat else None, n_rope_cols=2 * A_HEADS * A_VD)
                if lat:
                    ck = cache_k[:, j].reshape(nb_l, past, A_HEADS * A_VD)
                    cv = cache_v[:, j].reshape(nb_l, past, A_HEADS * A_VD)
                else:
                    ck = cv = None
                    hw = A_HEADS * A_VD
                    new_k.append(qkv[:, hw:2 * hw].reshape(nb_c, s_c, A_HEADS, 2, A_HD))
                    new_v.append(qkv[:, 2 * hw:].reshape(nb_c, s_c, A_HEADS, A_VD))
                a = _attention(qkv, ck, cv, attn_lam[j], attn_subln[j], batch=st["batch"], seq=st["seq"],
                               tq=min(512, st["seq"]), tk=256, lam_init=lam_init)
                mix[name] = (a, w_out)
        elif kind == 1:
            n_main = SSD_INNER + SSD_INNER + 2 * SSD_GROUPS * SSD_STATE
            w_main = ssd_w_in[j][:, :n_main].astype(BF16)
            w_dt = _pad_cols(ssd_w_in[j][:, n_main:], LANES).astype(BF16)
            w_out = ssd_w_out[j].astype(BF16)
            par = _pad_cols(jnp.stack([ssd_a_log[j].reshape(-1), ssd_dt_bias[j].reshape(-1)]), LANES)
            dskip = jnp.repeat(ssd_d[j], SSD_HD)
            for name, st in sets.items():
                zx = _in_proj(xs[name], mods[name], w_main, seq=st["seq"], out_dtype=BF16, tm=st["tp"], tn=1024)
                dt = _in_proj(xs[name], mods[name], w_dt, seq=st["seq"], out_dtype=F32, tm=st["tp"], tn=LANES)
                if name == "lat":
                    s0 = state_ssd[:, j].reshape(nb_l, 2, SSD_INNER, SSD_STATE)
                else:
                    s0 = jnp.zeros((nb_c, 2, SSD_INNER, SSD_STATE), F32)
                y, s_fin = _ssd_scan(zx, dt, ssd_conv_w[j], ssd_conv_b[j], par, dskip,
                                     ssd_norm[j], s0, batch=st["batch"], seq=st["seq"])
                if name == "ctx":
                    new_ssd.append(s_fin.reshape(nb_c, 2, SSD_HEADS, SSD_HD, SSD_STATE))
                mix[name] = (y, w_out)
        else:
            n_main = 3 * GDN_HEADS * GDN_DK + GDN_HEADS * GDN_DV
            w_main = gdn_w_in[j][:, :n_main].astype(BF16)
            w_ab = _pad_cols(gdn_w_in[j][:, n_main:], LANES).astype(BF16)
            w_out = gdn_w_out[j].astype(BF16)
            zeros = jnp.zeros((2 * GDN_HEADS,), F32)
            par = jnp.stack([jnp.concatenate([gdn_a_log[j].reshape(-1), zeros]),
                             jnp.concatenate([gdn_dt_bias[j].reshape(-1), zeros])])
            for name, st in sets.items():
                qkvz = _in_proj(xs[name], mods[name], w_main, seq=st["seq"], out_dtype=BF16, tm=st["tp"], tn=1024)
                ab = _in_proj(xs[name], mods[name], w_ab, seq=st["seq"], out_dtype=F32, tm=st["tp"], tn=LANES)
                if name == "lat":
                    s0 = state_delta[:, j]
                else:
                    s0 = jnp.zeros((nb_c, 2, GDN_HEADS, GDN_DK, GDN_DV), F32)
                y, s_fin = _gdn_scan(qkvz, ab[:, :4 * GDN_HEADS], gdn_conv_w[j], par, gdn_norm[j], s0,
                                     batch=st["batch"], seq=st["seq"])
                if name == "ctx":
                    new_gdn.append(s_fin)
                mix[name] = (y, w_out)

        w_router = _pad_cols(jnp.concatenate([moe_w_re[i], moe_w_rg[i]], axis=1), LANES)
        b_router = _pad_cols(jnp.concatenate([moe_b_re[i], moe_b_rg[i]])[None, :], LANES)
        w_gu = jnp.concatenate([moe_w_gate[i], moe_w_up[i]], axis=-1).astype(BF16)
        w_d = moe_w_down[i].astype(BF16)
        for name, st in sets.items():
            a, w_out = mix[name]
            x1 = _out_proj(a, xs[name], mods[name], w_out, ln_g[i, 0], ln_b[i, 0], seq=st["seq"], tm=st["tm"])
            xs[name] = _moe(x1, mods[name], w_router, b_router, w_gu, w_d, ln_g[i, 1], ln_b[i, 1],
                            seq=st["seq"], tm=st["tp"])

    return (xs["ctx"].reshape(nb_c, s_c, d), xs["lat"].reshape(nb_l, s_l, d),
            jnp.stack(new_k, axis=1), jnp.stack(new_v, axis=1),
            jnp.stack(new_ssd, axis=1), jnp.stack(new_gdn, axis=1))
```

```python
import functools
import math

import jax
import jax.numpy as jnp
from jax import lax
from jax.experimental import pallas as pl
from jax.experimental.pallas import tpu as pltpu

F32 = jnp.float32
BF16 = jnp.bfloat16
HIGHEST = lax.Precision.HIGHEST

D_MODEL = 1024
DEPTH = 4
GRID_W = 64
DN_ALPHA = (2.0 * DEPTH) ** 0.25
EPS = 1e-6
A_HEADS = 8
A_HD = 64
A_VD = 2 * A_HD
ROPE_BASE = 10000.0
SSD_INNER = 2 * D_MODEL
SSD_HD = 64
SSD_HEADS = SSD_INNER // SSD_HD
SSD_GROUPS = 8
SSD_STATE = 128
SSD_GROUP_W = SSD_INNER // SSD_GROUPS
GDN_HEADS = 8
GDN_DK = 128
GDN_DV = 128
CONV_W = 5
CHUNK = 64
MOE_GROUPS = 4
MOE_PER_GROUP = 4
MOE_EXPERTS = MOE_GROUPS * MOE_PER_GROUP
MOE_FF = 256
MOE_EXPERT_BLOCK = 4

LANES = 128
HALO = 16
VMEM_LIMIT = 52 * 1024 * 1024


def _params(*sem):
    return pltpu.CompilerParams(dimension_semantics=sem, vmem_limit_bytes=VMEM_LIMIT)


def _silu(v):
    return v * jax.nn.sigmoid(v)


def _softplus(v):
    return jnp.maximum(v, 0.0) + jnp.log(1.0 + jnp.exp(-jnp.abs(v)))


def _layer_norm(v, g, b):
    mu = jnp.mean(v, axis=-1, keepdims=True)
    d = v - mu
    var = jnp.mean(d * d, axis=-1, keepdims=True)
    return d * lax.rsqrt(var + EPS) * g + b


def _dot(a, b):
    return jnp.dot(a, b, preferred_element_type=F32)


def _dot_hi(a, b):
    return jnp.dot(a, b, preferred_element_type=F32, precision=HIGHEST)


def _dot_nt(a, b):
    return lax.dot_general(a, b, (((1,), (1,)), ((), ())), preferred_element_type=F32)


def _dot_tn(a, b):
    return lax.dot_general(a, b, (((0,), (0,)), ((), ())), preferred_element_type=F32)


def _mod_kernel(c_ref, w_ref, b_ref, o_ref):
    o_ref[...] = _dot_hi(_silu(c_ref[...]), w_ref[...]) + b_ref[...]


def _modulations(cond, mod_w, mod_b):
    depth, d, n = mod_w.shape
    r = cond.shape[0]
    tn = 1536
    return pl.pallas_call(
        _mod_kernel,
        grid=(depth, n // tn),
        in_specs=[pl.BlockSpec((r, d), lambda l, j: (0, 0)),
                  pl.BlockSpec((None, d, tn), lambda l, j: (l, 0, j)),
                  pl.BlockSpec((None, 1, tn), lambda l, j: (l, 0, j))],
        out_specs=pl.BlockSpec((None, r, tn), lambda l, j: (l, 0, j)),
        out_shape=jax.ShapeDtypeStruct((depth, r, n), F32),
        compiler_params=_params("arbitrary", "arbitrary"),
        name="modulation",
    )(cond, mod_w, mod_b.reshape(depth, 1, n))


def _in_proj_kernel(*refs, n_rope_tiles, tn):
    if n_rope_tiles:
        x_ref, mod_ref, w_ref, cos_ref, sa_ref, sb_ref, o_ref, h_scr = refs
    else:
        x_ref, mod_ref, w_ref, o_ref, h_scr = refs
    j = pl.program_id(1)

    @pl.when(j == 0)
    def _():
        h = x_ref[...] * (1.0 + mod_ref[1:2, :]) + mod_ref[0:1, :]
        h_scr[...] = h.astype(BF16)

    acc = _dot(h_scr[...], w_ref[...])
    if not n_rope_tiles:
        o_ref[...] = acc.astype(o_ref.dtype)
        return

    @pl.when(j < n_rope_tiles)
    def _():
        cos, sa, sb = cos_ref[...], sa_ref[...], sb_ref[...]
        for c in range(tn // LANES):
            a = acc[:, c * LANES:(c + 1) * LANES]
            r = a * cos + pltpu.roll(a, LANES - A_HD // 4, 1) * sa + pltpu.roll(a, A_HD // 4, 1) * sb
            o_ref[:, c * LANES:(c + 1) * LANES] = r.astype(o_ref.dtype)

    @pl.when(j >= n_rope_tiles)
    def _():
        o_ref[...] = acc.astype(o_ref.dtype)


def _in_proj(x, mods, w, *, seq, out_dtype, tm, tn, rope=None, n_rope_cols=0):
    t, d = x.shape
    n = w.shape[1]
    shared = mods.shape[0] == 1
    assert shared or seq % tm == 0
    tiles_per_req = max(seq // tm, 1)
    mod_map = (lambda i, j: (0, 0, 0)) if shared else (lambda i, j: (i // tiles_per_req, 0, 0))
    in_specs = [pl.BlockSpec((tm, d), lambda i, j: (i, 0)),
                pl.BlockSpec((None, 6, d), mod_map),
                pl.BlockSpec((d, tn), lambda i, j: (0, j))]
    args = [x, mods, w]
    n_rope_tiles = 0
    if rope is not None:
        n_rope_tiles = n_rope_cols // tn
        for tab in rope:
            in_specs.append(pl.BlockSpec((tm, LANES), lambda i, j: (i % tiles_per_req, 0)))
            args.append(tab)
    return pl.pallas_call(
        functools.partial(_in_proj_kernel, n_rope_tiles=n_rope_tiles, tn=tn),
        grid=(t // tm, n // tn),
        in_specs=in_specs,
        out_specs=pl.BlockSpec((tm, tn), lambda i, j: (i, j)),
        out_shape=jax.ShapeDtypeStruct((t, n), out_dtype),
        scratch_shapes=[pltpu.VMEM((tm, d), BF16)],
        compiler_params=_params("arbitrary", "arbitrary"),
        name="in_proj",
    )(*args)


def _out_proj_kernel(a_ref, x_ref, mod_ref, w_ref, g_ref, b_ref, o_ref):
    y = _dot(a_ref[...], w_ref[...])
    v = DN_ALPHA * x_ref[...] + mod_ref[2:3, :] * y
    o_ref[...] = _layer_norm(v, g_ref[...], b_ref[...])


def _out_proj(a, x, mods, w, ln_g, ln_b, *, seq, tm):
    t, d = x.shape
    k = a.shape[1]
    shared = mods.shape[0] == 1
    assert shared or seq % tm == 0
    tiles_per_req = max(seq // tm, 1)
    mod_map = (lambda i: (0, 0, 0)) if shared else (lambda i: (i // tiles_per_req, 0, 0))
    return pl.pallas_call(
        _out_proj_kernel,
        grid=(t // tm,),
        in_specs=[pl.BlockSpec((tm, k), lambda i: (i, 0)),
                  pl.BlockSpec((tm, d), lambda i: (i, 0)),
                  pl.BlockSpec((None, 6, d), mod_map),
                  pl.BlockSpec((k, d), lambda i: (0, 0)),
                  pl.BlockSpec((1, d), lambda i: (0, 0)),
                  pl.BlockSpec((1, d), lambda i: (0, 0))],
        out_specs=pl.BlockSpec((tm, d), lambda i: (i, 0)),
        out_shape=jax.ShapeDtypeStruct((t, d), F32),
        compiler_params=_params("arbitrary"),
        name="out_proj",
    )(a, x, mods, w, ln_g.reshape(1, d), ln_b.reshape(1, d))


def _moe_kernel(x_ref, mod_ref, wr_ref, br_ref, wgu_ref, wd_ref, g_ref, b_ref, o_ref,
                t_scr, gate_scr, acc_scr):
    e = pl.program_id(1)
    tm = x_ref.shape[0]
    lane = lax.broadcasted_iota(jnp.int32, (tm, LANES), 1)

    @pl.when(e == 0)
    def _():
        t = x_ref[...] * (1.0 + mod_ref[4:5, :]) + mod_ref[3:4, :]
        t_hi, t_lo = _hi_lo(t)
        t_scr[...] = t_hi
        w_hi, w_lo = _hi_lo(wr_ref[...])
        logits = _dot(t_hi, w_hi) + _dot(t_hi, w_lo) + _dot(t_lo, w_hi) + br_ref[...]
        lane_f = lane.astype(F32)
        neg = jnp.float32(-jnp.inf)
        big = jnp.float32(1e9)
        is_g = (lane >= MOE_EXPERTS) & (lane < MOE_EXPERTS + MOE_GROUPS)
        gl = jnp.where(is_g, logits, neg)
        gmax = jnp.max(gl, axis=-1, keepdims=True)
        pg_top = 1.0 / jnp.sum(jnp.exp(gl - gmax), axis=-1, keepdims=True)
        g_top = jnp.min(jnp.where(gl == gmax, lane_f, big), axis=-1, keepdims=True) - MOE_EXPERTS
        in_grp = (lane < MOE_EXPERTS) & ((lane // MOE_PER_GROUP).astype(F32) == g_top)
        el = jnp.where(in_grp, logits, neg)
        m1 = jnp.max(el, axis=-1, keepdims=True)
        i1 = jnp.min(jnp.where(el == m1, lane_f, big), axis=-1, keepdims=True)
        el2 = jnp.where(lane_f == i1, neg, el)
        m2 = jnp.max(el2, axis=-1, keepdims=True)
        i2 = jnp.min(jnp.where(el2 == m2, lane_f, big), axis=-1, keepdims=True)
        r = jnp.exp(m2 - m1)
        w1 = pg_top / (1.0 + r)
        w2 = pg_top * r / (1.0 + r)
        gate_scr[...] = jnp.where(lane_f == i1, w1, 0.0) + jnp.where(lane_f == i2, w2, 0.0)
        acc_scr[...] = jnp.zeros_like(acc_scr)

    gate = gate_scr[...]
    hids = []
    for k in range(MOE_EXPERT_BLOCK):
        hgu = _dot(t_scr[...], wgu_ref[k])
        g_e = jnp.sum(jnp.where(lane == e * MOE_EXPERT_BLOCK + k, gate, 0.0), axis=-1, keepdims=True)
        hids.append((_silu(hgu[:, :MOE_FF]) * hgu[:, MOE_FF:] * g_e).astype(BF16))
    acc_scr[...] += _dot(jnp.concatenate(hids, axis=1), wd_ref[...])

    @pl.when(e == MOE_EXPERTS // MOE_EXPERT_BLOCK - 1)
    def _():
        v = DN_ALPHA * x_ref[...] + mod_ref[5:6, :] * acc_scr[...]
        o_ref[...] = _layer_norm(v, g_ref[...], b_ref[...])


def _moe(x, mods, w_router, b_router, w_gu, w_d, ln_g, ln_b, *, seq, tm):
    t, d = x.shape
    shared = mods.shape[0] == 1
    assert shared or seq % tm == 0
    tiles_per_req = max(seq // tm, 1)
    mod_map = (lambda i, e: (0, 0, 0)) if shared else (lambda i, e: (i // tiles_per_req, 0, 0))
    return pl.pallas_call(
        _moe_kernel,
        grid=(t // tm, MOE_EXPERTS // MOE_EXPERT_BLOCK),
        in_specs=[pl.BlockSpec((tm, d), lambda i, e: (i, 0)),
                  pl.BlockSpec((None, 6, d), mod_map),
                  pl.BlockSpec((d, LANES), lambda i, e: (0, 0)),
                  pl.BlockSpec((1, LANES), lambda i, e: (0, 0)),
                  pl.BlockSpec((MOE_EXPERT_BLOCK, d, 2 * MOE_FF), lambda i, e: (e, 0, 0)),
                  pl.BlockSpec((None, MOE_EXPERT_BLOCK * MOE_FF, d), lambda i, e: (e, 0, 0)),
                  pl.BlockSpec((1, d), lambda i, e: (0, 0)),
                  pl.BlockSpec((1, d), lambda i, e: (0, 0))],
        out_specs=pl.BlockSpec((tm, d), lambda i, e: (i, 0)),
        out_shape=jax.ShapeDtypeStruct((t, d), F32),
        scratch_shapes=[pltpu.VMEM((tm, d), BF16), pltpu.VMEM((tm, LANES), F32), pltpu.VMEM((tm, d), F32)],
        compiler_params=_params("arbitrary", "arbitrary"),
        name="moe",
    )(x, mods, w_router, b_router, w_gu,
      w_d.reshape(MOE_EXPERTS // MOE_EXPERT_BLOCK, MOE_EXPERT_BLOCK * MOE_FF, d),
      ln_g.reshape(1, d), ln_b.reshape(1, d))


def _attn_kernel(*refs, n_seq_tiles, tk, tw, has_cache, lam_init):
    if has_cache:
        q_ref, k_ref, v_ref, ck_ref, cv_ref, lam_ref, sub_ref, o_ref, kt_scr, vb_scr, s_scr, acc_scr = refs
    else:
        q_ref, k_ref, v_ref, lam_ref, sub_ref, o_ref, kt_scr, vb_scr, s_scr, acc_scr = refs
    tq = q_ref.shape[0]
    off = tk if has_cache else 0
    n_keys = n_seq_tiles * tk + off

    @pl.when(pl.program_id(2) == 0)
    def _():
        if has_cache:
            kt_scr[:, 0:tk] = ck_ref[...].T.astype(BF16)
            vb_scr[0:tk, :] = cv_ref[...].astype(BF16)
        for i in range(n_seq_tiles):
            kt_scr[:, off + i * tk:off + (i + 1) * tk] = k_ref[i * tk:(i + 1) * tk, :].astype(F32).T.astype(BF16)
            vb_scr[off + i * tk:off + (i + 1) * tk, :] = v_ref[i * tk:(i + 1) * tk, :].astype(BF16)

    lane = lax.broadcasted_iota(jnp.int32, (tq, LANES), 1)
    q = q_ref[...].astype(F32) * (A_HD ** -0.5)
    qs = (jnp.where(lane < A_HD, q, 0.0).astype(BF16), jnp.where(lane >= A_HD, q, 0.0).astype(BF16))

    tiles = [(a, min(a + tw, n_keys)) for a in range(0, n_keys, tw)]
    ms = [jnp.full((tq, 1), -jnp.inf, F32)] * 2
    for a, b in tiles:
        k_tile = kt_scr[:, a:b]
        for c in range(2):
            s = _dot(qs[c], k_tile)
            s_scr[c, :, a:b] = s
            ms[c] = jnp.maximum(ms[c], jnp.max(s, axis=-1, keepdims=True))

    ls = [jnp.zeros((tq, 1), F32)] * 2
    for i, (a, b) in enumerate(tiles):
        v_tile = vb_scr[a:b, :]
        for c in range(2):
            p = jnp.exp(s_scr[c, :, a:b] - ms[c])
            ls[c] = ls[c] + jnp.sum(p, axis=-1, keepdims=True)
            pv = _dot(p.astype(BF16), v_tile)
            if i == 0:
                acc_scr[c] = pv
            else:
                acc_scr[c] += pv

    lp = lam_ref[...]
    lam = (jnp.exp(jnp.sum(lp[0:1, :] * lp[1:2, :], axis=-1, keepdims=True))
           - jnp.exp(jnp.sum(lp[2:3, :] * lp[3:4, :], axis=-1, keepdims=True)) + lam_init)
    o = acc_scr[0] / ls[0] - lam * (acc_scr[1] / ls[1])
    o = o * lax.rsqrt(jnp.mean(o * o, axis=-1, keepdims=True) + EPS) * sub_ref[...]
    o_ref[...] = (o * (1.0 - lam_init)).astype(o_ref.dtype)


def _attention(qkv, cache_k, cache_v, lam_p, subln, *, batch, seq, tq, tk, lam_init):
    has_cache = cache_k is not None
    hw = A_HEADS * A_VD
    nq = seq // tq
    n_seq_tiles = seq // tk
    in_specs = [pl.BlockSpec((tq, A_VD), lambda b, h, i: (b * nq + i, h)),
                pl.BlockSpec((seq, A_VD), lambda b, h, i: (b, A_HEADS + h)),
                pl.BlockSpec((seq, A_VD), lambda b, h, i: (b, 2 * A_HEADS + h))]
    args = [qkv, qkv, qkv]
    n_tiles = n_seq_tiles
    if has_cache:
        past = cache_k.shape[1]
        assert past == tk
        in_specs += [pl.BlockSpec((None, past, A_VD), lambda b, h, i: (b, 0, h)),
                     pl.BlockSpec((None, past, A_VD), lambda b, h, i: (b, 0, h))]
        args += [cache_k, cache_v]
        n_tiles += 1
    in_specs += [pl.BlockSpec((4, A_HD), lambda b, h, i: (0, 0)),
                 pl.BlockSpec((1, A_VD), lambda b, h, i: (0, 0))]
    args += [lam_p, subln.reshape(1, A_VD)]
    return pl.pallas_call(
        functools.partial(_attn_kernel, n_seq_tiles=n_seq_tiles, tk=tk, tw=tk, has_cache=has_cache,
                          lam_init=lam_init),
        grid=(batch, A_HEADS, nq),
        in_specs=in_specs,
        out_specs=pl.BlockSpec((tq, A_VD), lambda b, h, i: (b * nq + i, h)),
        out_shape=jax.ShapeDtypeStruct((batch * seq, hw), BF16),
        scratch_shapes=[pltpu.VMEM((A_VD, n_tiles * tk), BF16), pltpu.VMEM((n_tiles * tk, A_VD), BF16),
                        pltpu.VMEM((2, tq, n_tiles * tk), F32), pltpu.VMEM((2, tq, A_VD), F32)],
        compiler_params=_params("arbitrary", "arbitrary", "arbitrary"),
        name="diff_attention",
    )(*args)


def _rope_tables(seq):
    quarter = A_HD // 4
    inv_freq = ROPE_BASE ** (-jnp.arange(quarter, dtype=F32) / quarter)
    pos = jnp.arange(seq)
    row = (pos // GRID_W).astype(F32)
    col = (pos % GRID_W).astype(F32)
    lane = jnp.arange(LANES)
    d = lane % A_HD
    p = jnp.where((d < A_HD // 2)[None, :], row[:, None], col[:, None])
    ang = p * inv_freq[d % quarter][None, :]
    first = ((d % (A_HD // 2)) < quarter)[None, :]
    cos, sin = jnp.cos(ang), jnp.sin(ang)
    return cos, jnp.where(first, -sin, 0.0), jnp.where(first, 0.0, sin)


def _conv_silu_chunk(raw_ref, w_ref, bias, dst_ref, win_scr, c, n_chunks):
    width = raw_ref.shape[1]
    seq = n_chunks * CHUNK
    r0 = pl.multiple_of(c * CHUNK, CHUNK)
    prev = raw_ref[pl.ds(pl.multiple_of(jnp.maximum(r0 - HALO, 0), HALO), HALO), :].astype(F32)
    nxt = raw_ref[pl.ds(pl.multiple_of(jnp.minimum(r0 + CHUNK, seq - HALO), HALO), HALO), :].astype(F32)
    win_scr[0:HALO, 0:width] = jnp.where(c > 0, prev, 0.0)
    win_scr[HALO:HALO + CHUNK, 0:width] = raw_ref[pl.ds(r0, CHUNK), :].astype(F32)
    win_scr[HALO + CHUNK:, 0:width] = jnp.where(c < n_chunks - 1, nxt, 0.0)
    acc = jnp.zeros((CHUNK, width), F32) + bias
    for k in range(CONV_W):
        s = HALO - CONV_W // 2 + k
        acc = acc + w_ref[k:k + 1, :] * win_scr[s:s + CHUNK, 0:width]
    dst_ref[pl.ds(r0, CHUNK), :] = _silu(acc).astype(dst_ref.dtype)


def _tri(n, upper):
    i = lax.broadcasted_iota(jnp.int32, (n, n), 0)
    j = lax.broadcasted_iota(jnp.int32, (n, n), 1)
    return ((i <= j) if upper else (i >= j)).astype(F32)


def _split3(v):
    hi = v.astype(BF16)
    r1 = v - hi.astype(F32)
    mid = r1.astype(BF16)
    lo = (r1 - mid.astype(F32)).astype(BF16)
    return hi, mid, lo


def _chunk_sums(v):
    rows = v.shape[0]
    i = lax.broadcasted_iota(jnp.int32, (rows, rows), 0)
    j = lax.broadcasted_iota(jnp.int32, (rows, rows), 1)
    same = (i // CHUNK) == (j // CHUNK)
    lower = (same & (i >= j)).astype(BF16)
    upper = (same & (i <= j)).astype(BF16)
    pieces = _split3(v)
    return sum(_dot(lower, p) for p in pieces), sum(_dot(upper, p) for p in pieces)


def _ssd_gates_kernel(dt_ref, par_ref, o_ref):
    nh = 2 * SSD_HEADS
    dt = _softplus(dt_ref[...] + par_ref[1:2, :])
    pre, suf = _chunk_sums(dt * -jnp.exp(par_ref[0:1, :]))
    lane = lax.broadcasted_iota(jnp.int32, dt.shape, 1)
    cs = pltpu.roll(jnp.where(lane < SSD_HEADS, pre, suf), nh, 1)
    o_ref[...] = jnp.where(lane < nh, dt, cs)


def _ssd_gates(dt_raw, par):
    t, w = dt_raw.shape
    rows = 4 * CHUNK
    return pl.pallas_call(
        _ssd_gates_kernel,
        grid=(t // rows,),
        in_specs=[pl.BlockSpec((rows, w), lambda i: (i, 0)), pl.BlockSpec((2, w), lambda i: (0, 0))],
        out_specs=pl.BlockSpec((rows, w), lambda i: (i, 0)),
        out_shape=jax.ShapeDtypeStruct((t, w), F32),
        compiler_params=_params("arbitrary"),
        name="ssd_gates",
    )(dt_raw, par)


def _ssd_kernel(z_ref, x_ref, b_ref, c_ref, gates_ref, gt_ref, cwx_ref, cwb_ref, cwc_ref, cbx_ref, cbb_ref, cbc_ref,
                dskip_ref, nw_ref, s0_ref, y_ref, sout_ref,
                xc_scr, bc_scr, cc_scr, yf_scr, yb_scr, win_scr, *, n_chunks, unroll):
    g = pl.program_id(1)
    gw = SSD_GROUP_W
    hpg = SSD_HEADS // SSD_GROUPS
    nh = 2 * SSD_HEADS

    def conv_body(c, carry):
        _conv_silu_chunk(x_ref, cwx_ref, cbx_ref[...], xc_scr, win_scr, c, n_chunks)
        _conv_silu_chunk(b_ref, cwb_ref, cbb_ref[...], bc_scr, win_scr, c, n_chunks)
        _conv_silu_chunk(c_ref, cwc_ref, cbc_ref[...], cc_scr, win_scr, c, n_chunks)
        return carry

    lax.fori_loop(0, n_chunks, conv_body, 0)

    ii = lax.broadcasted_iota(jnp.int32, (CHUNK, gw), 0)
    jj = lax.broadcasted_iota(jnp.int32, (CHUNK, gw), 1) % CHUNK
    masks = ((ii >= jj), (ii <= jj))
    bi = lax.broadcasted_iota(jnp.int32, (gw, gw), 0)
    bj = lax.broadcasted_iota(jnp.int32, (gw, gw), 1)
    blockdiag = (bi // SSD_HD) == (bj // SSD_HD)
    sr = lax.broadcasted_iota(jnp.int32, (2 * nh, 2 * gw), 0)
    sl = lax.broadcasted_iota(jnp.int32, (2 * nh, 2 * gw), 1)
    sels = [(sr == (sl // gw) * nh + d * SSD_HEADS + g * hpg + (sl % gw) // SSD_HD).astype(BF16) for d in range(2)]
    sels = [jnp.concatenate([s, s], axis=0) for s in sels]

    def step(t, sts):
        chains = []
        for k in range(unroll):
            for d in range(2):
                c = t * unroll + k if d == 0 else n_chunks - 1 - (t * unroll + k)
                rows = pl.ds(pl.multiple_of(c * CHUNK, CHUNK), CHUNK)
                base = d * hpg
                dt_r = jnp.concatenate([gt_ref[c, base + hh:base + hh + 1, :] for hh in range(hpg)], axis=1)
                cs_r = jnp.concatenate([gt_ref[c, 2 * hpg + base + hh:2 * hpg + base + hh + 1, :]
                                        for hh in range(hpg)], axis=1)
                chains.append(dict(d=d, rows=rows, gt=gates_ref[rows, :], dt_r=dt_r, cs_r=cs_r,
                                   xq=xc_scr[rows, :], bq=bc_scr[rows, :], cq=cc_scr[rows, :]))
        for cn in chains:
            e = _dot(jnp.concatenate(_hi_lo(cn["gt"]), axis=1), sels[cn["d"]])
            cn["dt_e"], cn["cs_e"] = e[:, :gw], e[:, gw:]
            cn["cb"] = _dot_nt(cn["cq"], jnp.concatenate([cn["bq"]] * hpg, axis=0))
            cn["bt"] = cn["bq"].astype(F32).T.astype(BF16)
        for cn in chains:
            d, cs_e = cn["d"], cn["cs_e"]
            last = CHUNK - 1 if d == 0 else 0
            cs_last = cs_e[last:last + 1, :]
            m = jnp.where(masks[d], jnp.exp(cs_e - cn["cs_r"]), 0.0) * cn["dt_r"] * cn["cb"]
            xbd = jnp.concatenate([cn["xq"].astype(BF16)] * hpg, axis=0)
            xbd = jnp.where(blockdiag, xbd, jnp.zeros_like(xbd))
            cn["y"] = _dot(m.astype(BF16), xbd)
            xw = (cn["xq"] * (jnp.exp(cs_last - cs_e) * cn["dt_e"])).astype(BF16)
            cn["upd"] = _dot(cn["bt"], xw)
            cn["dec"] = jnp.exp(cs_last)
            cn["ecs"] = jnp.exp(cs_e)
        sts = list(sts)
        for cn in chains:
            d = cn["d"]
            y = cn["y"] + _dot(cn["cq"], sts[d].astype(BF16)) * cn["ecs"]
            sts[d] = sts[d] * cn["dec"] + cn["upd"]
            if d == 0:
                yf_scr[cn["rows"], :] = y
            else:
                yb_scr[cn["rows"], :] = y
        return tuple(sts)

    sts = lax.fori_loop(0, n_chunks // unroll, step, (s0_ref[0].T, s0_ref[1].T))

    for d in range(2):
        sout_ref[d] = sts[d].T

    fin_rows = 4 * CHUNK

    def fin_body(c, carry):
        rows = pl.ds(pl.multiple_of(c * fin_rows, fin_rows), fin_rows)
        y = (yf_scr[rows, :] + yb_scr[rows, :]) + dskip_ref[...] * xc_scr[rows, :]
        y = y * _silu(z_ref[rows, :].astype(F32))
        y = y * lax.rsqrt(jnp.mean(y * y, axis=-1, keepdims=True) + EPS) * nw_ref[...]
        y_ref[rows, :] = y.astype(y_ref.dtype)
        return carry

    lax.fori_loop(0, n_chunks * CHUNK // fin_rows, fin_body, 0)


def _ssd_scan(zx, dt_raw, conv_w, conv_b, par, dskip, norm_w, state0, *, batch, seq):
    gw = SSD_GROUP_W
    hpg = SSD_HEADS // SSD_GROUPS
    n_chunks = seq // CHUNK
    gates = _ssd_gates(dt_raw, par)
    gt = gates.reshape(batch * n_chunks, CHUNK, 2, 2, SSD_GROUPS, hpg).transpose(0, 4, 2, 3, 5, 1)
    gt = gt.reshape(batch * n_chunks, SSD_GROUPS, 4 * hpg, CHUNK)
    xb = SSD_INNER // gw
    bb = 2 * SSD_INNER // SSD_STATE
    cbk = bb + SSD_GROUPS
    cwb_off = SSD_INNER // SSD_STATE
    row = lambda width, off: pl.BlockSpec((seq, width), lambda r, g, off=off: (r, off + g))
    cw = lambda width, off: pl.BlockSpec((CONV_W, width), lambda r, g, off=off: (0, off + g))
    cb = lambda width, off: pl.BlockSpec((1, width), lambda r, g, off=off: (0, off + g))
    st_spec = pl.BlockSpec((None, 2, gw, SSD_STATE), lambda r, g: (r, 0, g, 0))
    conv_b2 = conv_b.reshape(1, -1)
    return pl.pallas_call(
        functools.partial(_ssd_kernel, n_chunks=n_chunks, unroll=4),
        grid=(batch, SSD_GROUPS),
        in_specs=[row(gw, 0), row(gw, xb), row(SSD_STATE, bb), row(SSD_STATE, cbk),
                  pl.BlockSpec((seq, LANES), lambda r, g: (r, 0)),
                  pl.BlockSpec((n_chunks, None, 4 * hpg, CHUNK), lambda r, g: (r, g, 0, 0)),
                  cw(gw, 0), cw(SSD_STATE, cwb_off), cw(SSD_STATE, cwb_off + SSD_GROUPS),
                  cb(gw, 0), cb(SSD_STATE, cwb_off), cb(SSD_STATE, cwb_off + SSD_GROUPS),
                  cb(gw, 0), cb(gw, 0), st_spec],
        out_specs=[pl.BlockSpec((seq, gw), lambda r, g: (r, g)), st_spec],
        out_shape=[jax.ShapeDtypeStruct((batch * seq, SSD_INNER), BF16),
                   jax.ShapeDtypeStruct(state0.shape, F32)],
        scratch_shapes=[pltpu.VMEM((seq, gw), F32), pltpu.VMEM((seq, SSD_STATE), BF16),
                        pltpu.VMEM((seq, SSD_STATE), BF16), pltpu.VMEM((seq, gw), F32),
                        pltpu.VMEM((seq, gw), F32),
                        pltpu.VMEM((CHUNK + 2 * HALO, gw), F32)],
        compiler_params=_params("arbitrary", "arbitrary"),
        name="ssd_scan",
    )(zx, zx, zx, zx, gates, gt, conv_w, conv_w, conv_w, conv_b2, conv_b2, conv_b2,
      dskip.reshape(1, -1), norm_w.reshape(1, -1), state0)


def _hi_lo(v):
    hi = v.astype(BF16)
    return hi, (v - hi.astype(F32)).astype(BF16)


def _lhs2(a):
    hi, lo = _hi_lo(a)
    return jnp.concatenate([hi, hi, lo], axis=1)


def _rhs2(b):
    hi, lo = _hi_lo(b)
    return jnp.concatenate([hi, lo, hi], axis=0)


def _gdn_gates_kernel(ab_ref, par_ref, o_ref):
    ab = ab_ref[...]
    pre, suf = _chunk_sums(-jnp.exp(par_ref[0:1, :]) * _softplus(ab + par_ref[1:2, :]))
    lane = lax.broadcasted_iota(jnp.int32, ab.shape, 1)
    o_ref[...] = jnp.where(lane < GDN_HEADS, pre, jnp.where(lane < 2 * GDN_HEADS, suf, jax.nn.sigmoid(ab)))


def _gdn_gates(ab_raw, par):
    t, w = ab_raw.shape
    rows = 4 * CHUNK
    return pl.pallas_call(
        _gdn_gates_kernel,
        grid=(t // rows,),
        in_specs=[pl.BlockSpec((rows, w), lambda i: (i, 0)), pl.BlockSpec((2, w), lambda i: (0, 0))],
        out_specs=pl.BlockSpec((rows, w), lambda i: (i, 0)),
        out_shape=jax.ShapeDtypeStruct((t, w), F32),
        compiler_params=_params("arbitrary"),
        name="gdn_gates",
    )(ab_raw, par)


def _gdn_kernel(q_ref, k_ref, v_ref, z_ref, gates_ref, gct_ref, cwq_ref, cwk_ref, cwv_ref, nw_ref, s0_ref,
                y_ref, sout_ref, qc_scr, kc_scr, vc_scr, w_scr, n_scr, qe_scr, el_scr,
                o_scr, win_scr, *, n_chunks, prep_unroll):
    h = pl.program_id(1)
    dk = GDN_DK

    def conv_body(c, carry):
        _conv_silu_chunk(q_ref, cwq_ref, 0.0, qc_scr, win_scr, c, n_chunks)
        _conv_silu_chunk(k_ref, cwk_ref, 0.0, kc_scr, win_scr, c, n_chunks)
        _conv_silu_chunk(v_ref, cwv_ref, 0.0, vc_scr, win_scr, c, n_chunks)
        r0 = pl.multiple_of(c * CHUNK, CHUNK)
        rows = pl.ds(r0, CHUNK)
        q = qc_scr[rows, :]
        qc_scr[rows, :] = q * lax.rsqrt(jnp.sum(q * q, axis=-1, keepdims=True) + EPS) * (dk ** -0.5)
        k = kc_scr[rows, :]
        kc_scr[rows, :] = k * lax.rsqrt(jnp.sum(k * k, axis=-1, keepdims=True) + EPS)
        return carry

    lax.fori_loop(0, n_chunks, conv_body, 0)

    ii = lax.broadcasted_iota(jnp.int32, (CHUNK, CHUNK), 0)
    jj = lax.broadcasted_iota(jnp.int32, (CHUNK, CHUNK), 1)
    eye = (ii == jj).astype(F32)
    glane = lax.broadcasted_iota(jnp.int32, (CHUNK, 4 * GDN_HEADS), 1)

    def prep(i, carry):
        chunks = []
        for u in range(prep_unroll):
            c = i * prep_unroll + u
            rows = pl.ds(pl.multiple_of(c * CHUNK, CHUNK), CHUNK)
            chunks.append(dict(c=c, rows=rows, q=qc_scr[rows, :], k=kc_scr[rows, :], v=vc_scr[rows, :],
                               gt=gates_ref[rows, :]))
        for ch in chunks:
            kb = ch["k"].astype(BF16)
            ch["kk"] = _dot_nt(kb, kb)
            ch["qk"] = _dot_nt(ch["q"].astype(BF16), kb)
        chains = []
        for ch in chunks:
            for d in range(2):
                idx = d * GDN_HEADS + h
                incl = (ii >= jj) if d == 0 else (ii <= jj)
                strict = (ii > jj) if d == 0 else (ii < jj)
                gt = ch["gt"]
                gc_col = jnp.sum(jnp.where(glane == idx, gt, 0.0), axis=1, keepdims=True)
                bt_col = jnp.sum(jnp.where(glane == 2 * GDN_HEADS + idx, gt, 0.0), axis=1, keepdims=True)
                gc_row = gct_ref[ch["c"], pl.ds(idx, 1), :]
                last = CHUNK - 1 if d == 0 else 0
                g_last = gc_col[last:last + 1, :]
                decay = jnp.where(incl, jnp.exp(gc_col - gc_row), 0.0)
                pw = -(jnp.where(strict, ch["kk"] * decay, 0.0) * bt_col)
                chains.append(dict(ch=ch, d=d, gc_col=gc_col, bt_col=bt_col, g_last=g_last, decay=decay,
                                   pw=pw, tinv=eye + pw))
        for _ in range(int(math.log2(CHUNK)) - 1):
            for cn in chains:
                cn["pw"] = _dot(_lhs2(cn["pw"]), _rhs2(cn["pw"]))
            for cn in chains:
                cn["tinv"] = cn["tinv"] + _dot(_lhs2(cn["tinv"]), _rhs2(cn["pw"]))
        for cn in chains:
            k, v = cn["ch"]["k"], cn["ch"]["v"]
            cn["egc"] = jnp.exp(cn["gc_col"])
            rhs = jnp.concatenate([k * (cn["bt_col"] * cn["egc"]), v * cn["bt_col"]], axis=1)
            cn["sol"] = _dot(_lhs2(cn["tinv"]), _rhs2(rhs))
        for cn in chains:
            ch = cn["ch"]
            kwt = (ch["k"] * jnp.exp(cn["g_last"] - cn["gc_col"])).T.astype(BF16)
            qkd = (ch["qk"] * cn["decay"]).astype(BF16)
            cn["prod"] = _dot(jnp.concatenate([kwt, qkd], axis=0), cn["sol"].astype(BF16))
        for cn in chains:
            ch, d, rows, prod = cn["ch"], cn["d"], cn["ch"]["rows"], cn["prod"]
            w_scr[d, ch["c"]] = prod[:dk, :dk].astype(BF16)
            n_scr[d, ch["c"]] = prod[:dk, dk:]
            qe_scr[d, rows, :] = (ch["q"] * cn["egc"] - prod[dk:, :dk]).astype(BF16)
            o_scr[d, rows, :] = prod[dk:, dk:]
            el_scr[d, ch["c"]] = jnp.broadcast_to(jnp.exp(cn["g_last"]), (8, LANES))
        return carry

    lax.fori_loop(0, n_chunks // prep_unroll, prep, 0)

    def step(t, sts):
        cs = (t, n_chunks - 1 - t)
        rows = [pl.ds(pl.multiple_of(c * CHUNK, CHUNK), CHUNK) for c in cs]
        dirs = range(2)
        stb = [sts[d].astype(BF16) for d in dirs]
        ws = [_dot(w_scr[d, cs[d]], stb[d]) for d in dirs]
        oq = [_dot(qe_scr[d, rows[d], :], stb[d]) for d in dirs]
        for d in dirs:
            o_scr[d, rows[d], :] += oq[d]
        return tuple(sts[d] * el_scr[d, cs[d]][0:1, :] - ws[d] + n_scr[d, cs[d]] for d in dirs)

    sts = lax.fori_loop(0, n_chunks, step, (s0_ref[0], s0_ref[1]))

    for d in range(2):
        sout_ref[d] = sts[d]

    fin_rows = 4 * CHUNK

    def fin_body(c, carry):
        rows = pl.ds(pl.multiple_of(c * fin_rows, fin_rows), fin_rows)
        o = o_scr[0, rows, :] + o_scr[1, rows, :]
        o = o * lax.rsqrt(jnp.mean(o * o, axis=-1, keepdims=True) + EPS) * nw_ref[...]
        y_ref[rows, :] = (o * _silu(z_ref[rows, :].astype(F32))).astype(y_ref.dtype)
        return carry

    lax.fori_loop(0, n_chunks * CHUNK // fin_rows, fin_body, 0)


def _gdn_scan(qkvz, ab_raw, conv_w, par, norm_w, state0, *, batch, seq):
    n_chunks = seq // CHUNK
    hh = GDN_HEADS
    gates = _gdn_gates(ab_raw, par)
    gct = gates[:, :2 * hh].reshape(batch * n_chunks, CHUNK, 2 * hh).transpose(0, 2, 1)
    row = lambda off: pl.BlockSpec((seq, GDN_DK), lambda r, h, off=off: (r, off + h))
    cw = lambda off: pl.BlockSpec((CONV_W, GDN_DK), lambda r, h, off=off: (0, off + h))
    st_spec = pl.BlockSpec((None, 2, None, GDN_DK, GDN_DV), lambda r, h: (r, 0, h, 0, 0))
    return pl.pallas_call(
        functools.partial(_gdn_kernel, n_chunks=n_chunks, prep_unroll=min(8, n_chunks)),
        grid=(batch, hh),
        in_specs=[row(0), row(hh), row(2 * hh), row(3 * hh),
                  pl.BlockSpec((seq, 4 * hh), lambda r, h: (r, 0)),
                  pl.BlockSpec((n_chunks, 2 * hh, CHUNK), lambda r, h: (r, 0, 0)),
                  cw(0), cw(hh), cw(2 * hh),
                  pl.BlockSpec((1, GDN_DV), lambda r, h: (0, 0)),
                  st_spec],
        out_specs=[pl.BlockSpec((seq, GDN_DV), lambda r, h: (r, h)), st_spec],
        out_shape=[jax.ShapeDtypeStruct((batch * seq, hh * GDN_DV), BF16),
                   jax.ShapeDtypeStruct(state0.shape, F32)],
        scratch_shapes=[pltpu.VMEM((seq, GDN_DK), F32), pltpu.VMEM((seq, GDN_DK), F32),
                        pltpu.VMEM((seq, GDN_DV), F32),
                        pltpu.VMEM((2, n_chunks, GDN_DK, GDN_DK), BF16), pltpu.VMEM((2, n_chunks, GDN_DK, GDN_DV), F32),
                        pltpu.VMEM((2, seq, GDN_DK), BF16), pltpu.VMEM((2, n_chunks, 8, LANES), F32),
                        pltpu.VMEM((2, seq, GDN_DV), F32),
                        pltpu.VMEM((CHUNK + 2 * HALO, GDN_DK), F32)],
        compiler_params=_params("arbitrary", "arbitrary"),
        name="gdn_scan",
    )(qkvz, qkvz, qkvz, qkvz, gates, gct, conv_w, conv_w, conv_w, norm_w.reshape(1, -1), state0)


def _pad_cols(w, n):
    return jnp.pad(w, ((0, 0), (0, n - w.shape[1])))


def kernel(x_prompt, x_sample, cache_k, cache_v, state_ssd, state_delta, c, c_ctx, mod_w, mod_b, ln_g, ln_b, attn_w_in, attn_lam, attn_subln, attn_w_out, ssd_w_in, ssd_conv_w, ssd_conv_b, ssd_a_log, ssd_dt_bias, ssd_d, ssd_norm, ssd_w_out, gdn_w_in, gdn_conv_w, gdn_a_log, gdn_dt_bias, gdn_norm, gdn_w_out, moe_w_rg, moe_b_rg, moe_w_re, moe_b_re, moe_w_gate, moe_w_up, moe_w_down):
    nb_c, s_c, d = x_prompt.shape
    nb_l, s_l, _ = x_sample.shape
    past = cache_k.shape[2]
    sets = {"ctx": dict(batch=nb_c, seq=s_c, tm=min(512, s_c), tp=min(1024, nb_c * s_c)),
            "lat": dict(batch=nb_l, seq=s_l, tm=512, tp=min(1024, s_l))}
    xs = {"ctx": x_prompt.reshape(nb_c * s_c, d), "lat": x_sample.reshape(nb_l * s_l, d)}

    n_cond = 8 * ((1 + nb_l + 7) // 8)
    cond = jnp.zeros((n_cond, d), F32).at[0].set(c_ctx).at[1:1 + nb_l].set(c)
    mods_all = _modulations(cond, mod_w, mod_b).reshape(-1, n_cond, 6, d)
    rope = _rope_tables(s_l)

    new_k, new_v, new_ssd, new_gdn = [], [], [], []
    for i in range(DEPTH):
        kind, j = i % 3, i // 3
        mods = {"ctx": mods_all[i, 0:1], "lat": mods_all[i, 1:1 + nb_l]}
        mix = {}
        if kind == 0:
            lam_init = 0.8 - 0.6 * math.exp(-0.3 * i)
            w_in = attn_w_in[j].astype(BF16)
            w_out = attn_w_out[j].astype(BF16)
            for name, st in sets.items():
                lat = name == "lat"
                qkv = _in_proj(xs[name], mods[name], w_in, seq=st["seq"], out_dtype=BF16 if lat else F32,
                               tm=st["tp"], tn=1024, rope=rope if lat else None, n_rope_cols=2 * A_HEADS * A_VD)
                if lat:
                    ck = cache_k[:, j].reshape(nb_l, past, A_HEADS * A_VD)
                    cv = cache_v[:, j].reshape(nb_l, past, A_HEADS * A_VD)
                else:
                    ck = cv = None
                    hw = A_HEADS * A_VD
                    new_k.append(qkv[:, hw:2 * hw].reshape(nb_c, s_c, A_HEADS, 2, A_HD))
                    new_v.append(qkv[:, 2 * hw:].reshape(nb_c, s_c, A_HEADS, A_VD))
                a = _attention(qkv, ck, cv, attn_lam[j], attn_subln[j], batch=st["batch"], seq=st["seq"],
                               tq=min(512, st["seq"]), tk=256, lam_init=lam_init)
                mix[name] = (a, w_out)
        elif kind == 1:
            n_main = SSD_INNER + SSD_INNER + 2 * SSD_GROUPS * SSD_STATE
            w_main = ssd_w_in[j][:, :n_main].astype(BF16)
            w_dt = _pad_cols(ssd_w_in[j][:, n_main:], LANES).astype(BF16)
            w_out = ssd_w_out[j].astype(BF16)
            par = _pad_cols(jnp.stack([ssd_a_log[j].reshape(-1), ssd_dt_bias[j].reshape(-1)]), LANES)
            dskip = jnp.repeat(ssd_d[j], SSD_HD)
            for name, st in sets.items():
                zx = _in_proj(xs[name], mods[name], w_main, seq=st["seq"], out_dtype=BF16, tm=st["tp"], tn=1024)
                dt = _in_proj(xs[name], mods[name], w_dt, seq=st["seq"], out_dtype=F32, tm=st["tp"], tn=LANES)
                if name == "lat":
                    s0 = state_ssd[:, j].reshape(nb_l, 2, SSD_INNER, SSD_STATE)
                else:
                    s0 = jnp.zeros((nb_c, 2, SSD_INNER, SSD_STATE), F32)
                y, s_fin = _ssd_scan(zx, dt, ssd_conv_w[j], ssd_conv_b[j], par, dskip,
                                     ssd_norm[j], s0, batch=st["batch"], seq=st["seq"])
                if name == "ctx":
                    new_ssd.append(s_fin.reshape(nb_c, 2, SSD_HEADS, SSD_HD, SSD_STATE))
                mix[name] = (y, w_out)
        else:
            n_main = 3 * GDN_HEADS * GDN_DK + GDN_HEADS * GDN_DV
            w_main = gdn_w_in[j][:, :n_main].astype(BF16)
            w_ab = _pad_cols(gdn_w_in[j][:, n_main:], LANES).astype(BF16)
            w_out = gdn_w_out[j].astype(BF16)
            zeros = jnp.zeros((2 * GDN_HEADS,), F32)
            par = jnp.stack([jnp.concatenate([gdn_a_log[j].reshape(-1), zeros]),
                             jnp.concatenate([gdn_dt_bias[j].reshape(-1), zeros])])
            for name, st in sets.items():
                qkvz = _in_proj(xs[name], mods[name], w_main, seq=st["seq"], out_dtype=BF16, tm=st["tp"], tn=1024)
                ab = _in_proj(xs[name], mods[name], w_ab, seq=st["seq"], out_dtype=F32, tm=st["tp"], tn=LANES)
                if name == "lat":
                    s0 = state_delta[:, j]
                else:
                    s0 = jnp.zeros((nb_c, 2, GDN_HEADS, GDN_DK, GDN_DV), F32)
                y, s_fin = _gdn_scan(qkvz, ab[:, :4 * GDN_HEADS], gdn_conv_w[j], par, gdn_norm[j], s0,
                                     batch=st["batch"], seq=st["seq"])
                if name == "ctx":
                    new_gdn.append(s_fin)
                mix[name] = (y, w_out)

        w_router = _pad_cols(jnp.concatenate([moe_w_re[i], moe_w_rg[i]], axis=1), LANES)
        b_router = _pad_cols(jnp.concatenate([moe_b_re[i], moe_b_rg[i]])[None, :], LANES)
        w_gu = jnp.concatenate([moe_w_gate[i], moe_w_up[i]], axis=-1).astype(BF16)
        w_d = moe_w_down[i].astype(BF16)
        for name, st in sets.items():
            a, w_out = mix[name]
            x1 = _out_proj(a, xs[name], mods[name], w_out, ln_g[i, 0], ln_b[i, 0], seq=st["seq"], tm=st["tm"])
            xs[name] = _moe(x1, mods[name], w_router, b_router, w_gu, w_d, ln_g[i, 1], ln_b[i, 1],
                            seq=st["seq"], tm=st["tp"])

    return (xs["ctx"].reshape(nb_c, s_c, d), xs["lat"].reshape(nb_l, s_l, d),
            jnp.stack(new_k, axis=1), jnp.stack(new_v, axis=1),
            jnp.stack(new_ssd, axis=1), jnp.stack(new_gdn, axis=1))
```
